```python
import math
import jax
import jax.numpy as jnp
from jax import lax
import numpy as np

D_MODEL = 1024
BATCH = 8
SEQ = 8192
DEPTH = 1
DEC_BATCH = 16
DEC_SEQ = 32
PAST_LEN = 1024

CHUNK = 64
EPS = 1e-6
D_INNER = 2 * D_MODEL
M_HEAD_DIM = 64
M_HEADS = D_INNER // M_HEAD_DIM
M_GROUPS = 4
M_HPG = M_HEADS // M_GROUPS
D_STATE = 128
CONV_W = 4
CONV_DIM = D_INNER + 2 * M_GROUPS * D_STATE
A_HEADS = 16
A_KV = 4
A_REP = A_HEADS // A_KV
A_HEAD_DIM = 64
A_WIDTH = A_HEADS * A_HEAD_DIM
WINDOW = 128
BAND = WINDOW // CHUNK
ROT_DIM = A_HEAD_DIM // 4
ROPE_THETA = 500000.0
ATTN_ROWS = min(WINDOW, PAST_LEN)
D_FF = 2816
IN_SIZES = (D_INNER, CONV_DIM, M_HEADS, A_WIDTH, A_KV * A_HEAD_DIM, A_KV * A_HEAD_DIM, D_MODEL, D_MODEL)
D_IN_PROJ = sum(IN_SIZES)
IN_SPLITS = tuple(int(s) for s in np.cumsum(IN_SIZES)[:-1])

kernel_name = 'macaron_ssd_swa_sink_gated_hybrid_stream_step'


def _rmsnorm(x, g):
    xf = x.astype(jnp.float32)
    y = xf * lax.rsqrt(jnp.mean(xf * xf, axis=-1, keepdims=True) + EPS)
    return (y * g.astype(jnp.float32)).astype(x.dtype)


def _swiglu(x, w_gu, w_down):
    gate, up = jnp.split(x @ w_gu, 2, axis=-1)
    return (jax.nn.silu(gate) * up) @ w_down


def _rope(x, pos):
    half = ROT_DIM // 2
    inv_freq = ROPE_THETA ** (-jnp.arange(0, ROT_DIM, 2, dtype=jnp.float32) / ROT_DIM)
    ang = pos.astype(jnp.float32)[:, None] * inv_freq[None, :]
    cos = jnp.cos(ang)[:, None, :]
    sin = jnp.sin(ang)[:, None, :]
    xf = x.astype(jnp.float32)
    x1, x2 = xf[..., :half], xf[..., half:ROT_DIM]
    out = jnp.concatenate([x1 * cos - x2 * sin, x2 * cos + x1 * sin, xf[..., ROT_DIM:]], axis=-1)
    return out.astype(x.dtype)


def _causal_conv(xbc, buf, w, b):
    L = xbc.shape[1]
    full = jnp.concatenate([buf.astype(xbc.dtype), xbc], axis=1)
    out = sum((full[:, j:j + L] * w[j] for j in range(CONV_W)), b)
    return jax.nn.silu(out), full[:, -(CONV_W - 1):]


def _ssd_chunk(xh, dt, Bm, Cm, A, h0):
    L = xh.shape[1]
    acs = jnp.cumsum(dt * A, axis=1)
    causal = jnp.tril(jnp.ones((L, L), dtype=bool))[None, :, :, None, None]
    seg = jnp.exp(jnp.where(causal, acs[:, :, None] - acs[:, None, :], -jnp.inf))
    xdt = xh * dt[..., None]
    cb = jnp.einsum('blgn,bsgn->blsg', Cm, Bm)
    y = jnp.einsum('blsgr,bsgrp->blgrp', cb[..., None] * seg, xdt)
    y = y + jnp.einsum('blgn,bgrpn->blgrp', Cm, h0) * jnp.exp(acs)[..., None]
    to_end = jnp.exp(acs[:, -1:] - acs)
    h = h0 * jnp.exp(acs[:, -1])[..., None, None] + jnp.einsum('blgrp,blgn->bgrpn', xdt * to_end[..., None], Bm)
    return y, h


def _mamba(z, xbc, dt_raw, conv_buf, h0, p):
    f32 = jnp.float32
    xbc, conv_out = _causal_conv(xbc, conv_buf, p['conv_w'], p['conv_b'])
    bsz, L = xbc.shape[:2]
    xs, Bm, Cm = jnp.split(xbc, [D_INNER, D_INNER + M_GROUPS * D_STATE], axis=-1)
    xh = xs.reshape(bsz, L, M_GROUPS, M_HPG, M_HEAD_DIM).astype(f32)
    Bm = Bm.reshape(bsz, L, M_GROUPS, D_STATE).astype(f32)
    Cm = Cm.reshape(bsz, L, M_GROUPS, D_STATE).astype(f32)
    dt = jax.nn.softplus(dt_raw.astype(f32) + p['dt_bias'].astype(f32)).reshape(bsz, L, M_GROUPS, M_HPG)
    A = -jnp.exp(p['a_log'].astype(f32)).reshape(M_GROUPS, M_HPG)
    h = h0.astype(f32).reshape(bsz, M_GROUPS, M_HPG, M_HEAD_DIM, D_STATE)
    cl = min(L, CHUNK)
    nc = L // cl

    def to_chunks(t):
        return jnp.moveaxis(t.reshape(bsz, nc, cl, *t.shape[2:]), 1, 0)

    def step(hc, inp):
        yc, hn = _ssd_chunk(*inp, A, hc)
        return hn, yc

    h_last, ys = lax.scan(step, h, (to_chunks(xh), to_chunks(dt), to_chunks(Bm), to_chunks(Cm)))
    y = jnp.moveaxis(ys, 0, 1).reshape(bsz, L, M_GROUPS, M_HPG, M_HEAD_DIM)
    y = y + p['d_skip'].astype(f32).reshape(M_GROUPS, M_HPG)[..., None] * xh
    y = y.reshape(bsz, L, D_INNER).astype(z.dtype)
    y = _rmsnorm(y * jax.nn.silu(z), p['m_norm_g'])
    ssm_out = h_last.reshape(bsz, M_HEADS, M_HEAD_DIM, D_STATE).astype(z.dtype)
    return y, conv_out, ssm_out


def _sink_attention(q, k, v, sink, key_valid):
    s = jnp.einsum('...qkrd,...skd->...krqs', q, k, preferred_element_type=jnp.float32) * (A_HEAD_DIM ** -0.5)
    if key_valid is not None:
        s = jnp.where(key_valid[..., None, None, None, :], s, -jnp.inf)
    sk = sink.astype(jnp.float32).reshape(A_KV, A_REP)[:, :, None, None]
    m = jnp.maximum(jnp.max(s, axis=-1, keepdims=True), sk)
    e = jnp.exp(s - m)
    prob = e / (jnp.sum(e, axis=-1, keepdims=True) + jnp.exp(sk - m))
    return jnp.einsum('...krqs,...skd->...qkrd', prob.astype(v.dtype), v)


def _attn_prompt(q, k, v, sink):
    bsz, L = q.shape[:2]
    nc = L // CHUNK
    qc = q.reshape(bsz, nc, CHUNK, A_KV, A_REP, A_HEAD_DIM)

    def band(t):
        tc = t.reshape(bsz, nc, CHUNK, A_KV, A_HEAD_DIM)
        tp = jnp.pad(tc, ((0, 0), (BAND, 0), (0, 0), (0, 0), (0, 0)))
        return jnp.concatenate([tp[:, j:j + nc] for j in range(BAND + 1)], axis=2)

    key_chunk = jnp.arange(nc)[:, None] - BAND + jnp.arange(BAND + 1)[None, :]
    valid = jnp.repeat(key_chunk >= 0, CHUNK, axis=1)
    o = _sink_attention(qc, band(k), band(v), sink, valid)
    return o.reshape(bsz, L, A_WIDTH)


def _attn_sample(q, k, v, k_cache, v_cache, sink):
    bsz, L = q.shape[:2]
    kk = jnp.concatenate([k_cache.astype(k.dtype), k], axis=1)
    vv = jnp.concatenate([v_cache.astype(v.dtype), v], axis=1)
    o = _sink_attention(q.reshape(bsz, L, A_KV, A_REP, A_HEAD_DIM), kk, vv, sink, None)
    return o.reshape(bsz, L, A_WIDTH)


def _layer(x, pos, state, p):
    x = x + 0.5 * _rmsnorm(_swiglu(_rmsnorm(x, p['ffn1_pre_g']), p['ffn1_w_gu'], p['ffn1_w_down']), p['ffn1_post_g'])
    u = _rmsnorm(x, p['mix_pre_g'])
    bsz, L = u.shape[:2]
    z, xbc, dt_raw, q, k, v, g_m, g_a = jnp.split(u @ p['w_in'], IN_SPLITS, axis=-1)
    q = _rope(q.reshape(bsz, L, A_HEADS, A_HEAD_DIM), pos)
    k = _rope(k.reshape(bsz, L, A_KV, A_HEAD_DIM), pos)
    v = v.reshape(bsz, L, A_KV, A_HEAD_DIM)
    if state is None:
        conv_buf = jnp.zeros((bsz, CONV_W - 1, CONV_DIM), u.dtype)
        h0 = jnp.zeros((bsz, M_HEADS, M_HEAD_DIM, D_STATE), jnp.float32)
        attn = _attn_prompt(q, k, v, p['attn_sink'])
        rows = min(WINDOW, L)
        k_out, v_out = k[:, L - rows:], v[:, L - rows:]
    else:
        conv_buf, h0, k_cache, v_cache = state
        attn = _attn_sample(q, k, v, k_cache, v_cache, p['attn_sink'])
        k_out, v_out = k, v
    y_m, conv_out, ssm_out = _mamba(z, xbc, dt_raw, conv_buf, h0, p)
    mixed = jax.nn.sigmoid(g_m) * (y_m @ p['w_br_m']) + jax.nn.sigmoid(g_a) * (attn @ p['w_br_a'])
    x = x + _rmsnorm(mixed @ p['w_o'], p['mix_post_g'])
    x = x + 0.5 * _rmsnorm(_swiglu(_rmsnorm(x, p['ffn2_pre_g']), p['ffn2_w_gu'], p['ffn2_w_down']), p['ffn2_post_g'])
    return x, (conv_out, ssm_out, k_out, v_out)


def setup_inputs(seed: int = 0) -> dict:
    key = jax.random.key(seed)
    ks = iter(jax.random.split(key, 32))
    f32 = jnp.float32

    def nrm(shape, scale):
        return jax.random.normal(next(ks), shape, f32) * scale

    def gain(shape):
        return 1.0 + nrm(shape, 0.02)

    dt_u = jax.random.uniform(next(ks), (DEPTH, M_HEADS), f32)
    dt0 = jnp.exp(dt_u * (math.log(0.1) - math.log(0.001)) + math.log(0.001))
    dt_bias = dt0 + jnp.log(-jnp.expm1(-dt0))
    a_log = jnp.log(jax.random.uniform(next(ks), (DEPTH, M_HEADS), f32, minval=1.0, maxval=16.0))
    return {
        'x_prompt': nrm((BATCH, SEQ, D_MODEL), 1.0),
        'x_sample': nrm((DEC_BATCH, DEC_SEQ, D_MODEL), 1.0),
        'state_conv': nrm((DEPTH, DEC_BATCH, CONV_W - 1, CONV_DIM), 1.0),
        'state_ssm': nrm((DEPTH, DEC_BATCH, M_HEADS, M_HEAD_DIM, D_STATE), 0.1),
        'cache_k': nrm((DEPTH, DEC_BATCH, ATTN_ROWS, A_KV, A_HEAD_DIM), 1.0),
        'cache_v': nrm((DEPTH, DEC_BATCH, ATTN_ROWS, A_KV, A_HEAD_DIM), 1.0),
        'ffn1_pre_g': gain((DEPTH, D_MODEL)),
        'ffn1_w_gu': nrm((DEPTH, D_MODEL, 2 * D_FF), D_MODEL ** -0.5),
        'ffn1_w_down': nrm((DEPTH, D_FF, D_MODEL), D_FF ** -0.5),
        'ffn1_post_g': gain((DEPTH, D_MODEL)),
        'mix_pre_g': gain((DEPTH, D_MODEL)),
        'w_in': nrm((DEPTH, D_MODEL, D_IN_PROJ), D_MODEL ** -0.5),
        'conv_w': nrm((DEPTH, CONV_W, CONV_DIM), CONV_W ** -0.5),
        'conv_b': nrm((DEPTH, CONV_DIM), 0.02),
        'dt_bias': dt_bias,
        'a_log': a_log,
        'd_skip': gain((DEPTH, M_HEADS)),
        'm_norm_g': gain((DEPTH, D_INNER)),
        'attn_sink': nrm((DEPTH, A_HEADS), 1.0),
        'w_br_m': nrm((DEPTH, D_INNER, D_MODEL), D_INNER ** -0.5),
        'w_br_a': nrm((DEPTH, A_WIDTH, D_MODEL), A_WIDTH ** -0.5),
        'w_o': nrm((DEPTH, D_MODEL, D_MODEL), D_MODEL ** -0.5),
        'mix_post_g': gain((DEPTH, D_MODEL)),
        'ffn2_pre_g': gain((DEPTH, D_MODEL)),
        'ffn2_w_gu': nrm((DEPTH, D_MODEL, 2 * D_FF), D_MODEL ** -0.5),
        'ffn2_w_down': nrm((DEPTH, D_FF, D_MODEL), D_FF ** -0.5),
        'ffn2_post_g': gain((DEPTH, D_MODEL)),
    }


def reference(x_prompt, x_sample, state_conv, state_ssm, cache_k, cache_v,
              ffn1_pre_g, ffn1_w_gu, ffn1_w_down, ffn1_post_g,
              mix_pre_g, w_in, conv_w, conv_b, dt_bias, a_log, d_skip, m_norm_g,
              attn_sink, w_br_m, w_br_a, w_o, mix_post_g,
              ffn2_pre_g, ffn2_w_gu, ffn2_w_down, ffn2_post_g):
    pos_p = jnp.arange(x_prompt.shape[1], dtype=jnp.int32)
    pos_s = PAST_LEN + jnp.arange(x_sample.shape[1], dtype=jnp.int32)
    yp, ys = x_prompt, x_sample
    new_p, new_s = [], []
    for l in range(DEPTH):
        p = {
            'ffn1_pre_g': ffn1_pre_g[l], 'ffn1_w_gu': ffn1_w_gu[l], 'ffn1_w_down': ffn1_w_down[l],
            'ffn1_post_g': ffn1_post_g[l], 'mix_pre_g': mix_pre_g[l], 'w_in': w_in[l],
            'conv_w': conv_w[l], 'conv_b': conv_b[l], 'dt_bias': dt_bias[l], 'a_log': a_log[l],
            'd_skip': d_skip[l], 'm_norm_g': m_norm_g[l], 'attn_sink': attn_sink[l],
            'w_br_m': w_br_m[l], 'w_br_a': w_br_a[l], 'w_o': w_o[l], 'mix_post_g': mix_post_g[l],
            'ffn2_pre_g': ffn2_pre_g[l], 'ffn2_w_gu': ffn2_w_gu[l], 'ffn2_w_down': ffn2_w_down[l],
            'ffn2_post_g': ffn2_post_g[l],
        }
        yp, st_p = _layer(yp, pos_p, None, p)
        ys, st_s = _layer(ys, pos_s, (state_conv[l], state_ssm[l], cache_k[l], cache_v[l]), p)
        new_p.append(st_p)
        new_s.append(st_s)
    conv_p, ssm_p, k_p, v_p = [jnp.stack(t) for t in zip(*new_p)]
    conv_s, ssm_s, k_s, v_s = [jnp.stack(t) for t in zip(*new_s)]
    return (yp, ys, conv_p, ssm_p, k_p, v_p, conv_s, ssm_s, k_s, v_s)
```

```python
import functools
import math

import jax
import jax.numpy as jnp
import numpy as np
from jax import lax
from jax.experimental import pallas as pl
from jax.experimental.pallas import tpu as pltpu

D_MODEL = 1024
CHUNK = 64
EPS = 1e-6
D_INNER = 2 * D_MODEL
M_HEAD_DIM = 64
M_HEADS = D_INNER // M_HEAD_DIM
M_GROUPS = 4
M_HPG = M_HEADS // M_GROUPS
D_STATE = 128
CONV_W = 4
CONV_DIM = D_INNER + 2 * M_GROUPS * D_STATE
A_HEADS = 16
A_KV = 4
A_REP = A_HEADS // A_KV
A_HEAD_DIM = 64
A_WIDTH = A_HEADS * A_HEAD_DIM
KV_WIDTH = A_KV * A_HEAD_DIM
WINDOW = 128
ROT_DIM = A_HEAD_DIM // 4
ROPE_THETA = 500000.0
D_FF = 2816
PAST_LEN = 1024
IN_SIZES = (D_INNER, CONV_DIM, M_HEADS, A_WIDTH, KV_WIDTH, KV_WIDTH, D_MODEL, D_MODEL)

LANES = 128
SUBLANES = 8
VMEM_LIMIT_BYTES = 60 * 1024 * 1024

HP = LANES
GW = M_HPG * M_HEAD_DIM
CONV_PAD = SUBLANES

BF16 = jnp.bfloat16
F32 = jnp.float32


def _dot(a, b):
    return jnp.dot(a, b, preferred_element_type=F32)


def _dot_nt(a, b):
    return lax.dot_general(a, b, (((1,), (1,)), ((), ())), preferred_element_type=F32)


def _rms(x, g):
    return x * lax.rsqrt(jnp.mean(x * x, axis=-1, keepdims=True) + EPS) * g


def _silu(x):
    return x * (1.0 / (1.0 + jnp.exp(-x)))


def _sigmoid(x):
    return 1.0 / (1.0 + jnp.exp(-x))


def _softplus(x):
    return jnp.maximum(x, 0.0) + jnp.log(1.0 + jnp.exp(-jnp.abs(x)))


def _const_spec(shape):
    zeros = (0,) * len(shape)
    return pl.BlockSpec(shape, lambda *_: zeros, pipeline_mode=pl.Buffered(1))


def _ffn_kernel(x_ref, gpre_ref, wg_ref, wu_ref, wd_ref, gpost_ref, o_ref):
    x = x_ref[...]
    xn = _rms(x, gpre_ref[...]).astype(BF16)
    gate = _dot(xn, wg_ref[...])
    up = _dot(xn, wu_ref[...])
    a = (_silu(gate) * up).astype(BF16)
    y = _dot(a, wd_ref[...])
    o_ref[...] = x + 0.5 * _rms(y, gpost_ref[...])


def _ffn(x2d, gpre, wg, wu, wd, gpost, tm):
    m = x2d.shape[0]
    assert m % tm == 0
    return pl.pallas_call(
        _ffn_kernel,
        grid=(m // tm,),
        in_specs=[
            pl.BlockSpec((tm, D_MODEL), lambda i: (i, 0)),
            _const_spec((1, D_MODEL)),
            _const_spec((D_MODEL, D_FF)),
            _const_spec((D_MODEL, D_FF)),
            _const_spec((D_FF, D_MODEL)),
            _const_spec((1, D_MODEL)),
        ],
        out_specs=pl.BlockSpec((tm, D_MODEL), lambda i: (i, 0)),
        out_shape=jax.ShapeDtypeStruct((m, D_MODEL), F32),
        compiler_params=pltpu.CompilerParams(
            dimension_semantics=("arbitrary",), vmem_limit_bytes=VMEM_LIMIT_BYTES),
        name="ffn_half_step",
    )(x2d, gpre, wg, wu, wd, gpost)


def _rope(x, cos_t, sa_t, sb_t):
    w = x.shape[1]
    reps = w // LANES
    cos_f = jnp.tile(cos_t, (1, reps))
    sa_f = jnp.tile(sa_t, (1, reps))
    sb_f = jnp.tile(sb_t, (1, reps))
    half = ROT_DIM // 2
    return x * cos_f + pltpu.roll(x, w - half, 1) * sa_f + pltpu.roll(x, half, 1) * sb_f


def _transpose_rows_padded(x):
    r = x.shape[0]
    rp = -(-r // LANES) * LANES
    if rp != r:
        x = jnp.concatenate([x, jnp.zeros((rp - r, x.shape[1]), x.dtype)], axis=0)
    xt = x.T
    return xt if rp == r else xt[:, :r]


def _mixer_kernel(*refs, tm, q, n_keep, has_state):
    it = iter(refs)
    x_ref = next(it)
    cos_ref, sa_ref, sb_ref = next(it), next(it), next(it)
    if has_state:
        conv0_ref, ssm0_ref, ck_ref, cv_ref = next(it), next(it), next(it), next(it)
    (gpre_ref, wz_ref, wxbc_ref, wdt_ref, wq_ref, wk_ref, wv_ref, wgm_ref, wga_ref,
     convw_ref, convb_ref, dtb_ref, arow_ref, dskip_ref, mng_ref, sink_ref,
     wbm_ref, wba_ref, wo_ref, gpost_ref) = [next(it) for _ in range(20)]
    y_ref, convo_ref, ssmo_ref, ko_ref, vo_ref = [next(it) for _ in range(5)]
    (xn_s, cbuf_s, xc_s, dt_s, acs_s, acst_s, dtt_s, bt_s, q_s, kwin_s, vwin_s,
     ht_s, ym_s, at_s) = [next(it) for _ in range(14)]

    si = pl.program_id(1)
    ns = pl.num_programs(1)
    nch = tm // q
    wlen = WINDOW + q

    @pl.when(si == 0)
    def _():
        if has_state:
            cbuf_s[CONV_PAD - (CONV_W - 1):CONV_PAD, :] = conv0_ref[0]
            ht_s[...] = ssm0_ref[0]
            kwin_s[0:WINDOW, :] = ck_ref[0].astype(BF16)
            vwin_s[0:WINDOW, :] = cv_ref[0].astype(BF16)
        else:
            cbuf_s[0:CONV_PAD, :] = jnp.zeros((CONV_PAD, CONV_DIM), F32)
            ht_s[...] = jnp.zeros(ht_s.shape, F32)
            kwin_s[0:WINDOW, :] = jnp.zeros((WINDOW, KV_WIDTH), BF16)
            vwin_s[0:WINDOW, :] = jnp.zeros((WINDOW, KV_WIDTH), BF16)

    x = x_ref[...]
    xn = _rms(x, gpre_ref[...]).astype(BF16)
    xn_s[...] = xn

    cbuf_s[CONV_PAD:CONV_PAD + tm, :] = _dot(xn, wxbc_ref[...])
    acc = convb_ref[...] + cbuf_s[CONV_PAD:CONV_PAD + tm, :] * convw_ref[CONV_W - 1:CONV_W, :]
    for j in range(CONV_W - 1):
        off = CONV_PAD - (CONV_W - 1) + j
        acc = acc + cbuf_s[off:off + tm, :] * convw_ref[j:j + 1, :]
    xc_s[...] = _silu(acc)
    tail = cbuf_s[CONV_PAD + tm - (CONV_W - 1):CONV_PAD + tm, :]
    convo_ref[0] = tail
    cbuf_s[CONV_PAD - (CONV_W - 1):CONV_PAD, :] = tail

    dt = _softplus(_dot(xn, wdt_ref[...]) + dtb_ref[...])
    dt_s[...] = dt
    dta = dt * arow_ref[...]
    hi = dta.astype(BF16)
    r1 = dta - hi.astype(F32)
    mid = r1.astype(BF16)
    lo = (r1 - mid.astype(F32)).astype(BF16)
    rr = lax.broadcasted_iota(jnp.int32, (tm, tm), 0)
    cc = lax.broadcasted_iota(jnp.int32, (tm, tm), 1)
    tri = jnp.where((rr >= cc) & ((rr // q) == (cc // q)), 1.0, 0.0).astype(BF16)
    cs3 = _dot(tri, jnp.concatenate([hi, mid, lo], axis=1))
    acs = cs3[:, 0:HP] + cs3[:, HP:2 * HP] + cs3[:, 2 * HP:3 * HP]
    acs_s[...] = acs
    acst = _transpose_rows_padded(acs)
    dtt = _transpose_rows_padded(dt)
    btt = _transpose_rows_padded(xc_s[:, D_INNER:D_INNER + M_GROUPS * D_STATE])
    for c in range(nch):
        acst_s[c] = acst[:, c * q:(c + 1) * q]
        dtt_s[c] = dtt[:, c * q:(c + 1) * q]
        bt_s[c] = btt[:, c * q:(c + 1) * q].astype(BF16)

    cos_t, sa_t, sb_t = cos_ref[...], sa_ref[...], sb_ref[...]
    qv = _rope(_dot(xn, wq_ref[...]), cos_t, sa_t, sb_t)
    q_s[...] = (qv * (A_HEAD_DIM ** -0.5)).astype(BF16)
    kv = _rope(_dot(xn, wk_ref[...]), cos_t, sa_t, sb_t)
    vv = _dot(xn, wv_ref[...])
    kwin_s[WINDOW:WINDOW + tm, :] = kv.astype(BF16)
    vwin_s[WINDOW:WINDOW + tm, :] = vv.astype(BF16)
    if has_state:
        ko_ref[0] = kv
        vo_ref[0] = vv
    else:
        @pl.when(si == ns - 1)
        def _():
            ko_ref[0] = kv[tm - n_keep:, :]
            vo_ref[0] = vv[tm - n_keep:, :]

    row_i = lax.broadcasted_iota(jnp.int32, (q, q), 0)
    col_i = lax.broadcasted_iota(jnp.int32, (q, q), 1)
    causal = row_i >= col_i
    key_j = lax.broadcasted_iota(jnp.int32, (1, wlen), 1)
    t0 = si * tm

    def chunk_body(c, carry):
        r0 = pl.multiple_of(c * q, q)
        acs_c = acs_s[pl.ds(r0, q), :]
        dt_c = dt_s[pl.ds(r0, q), :]
        acst_c = acst_s[c]
        dtt_c = dtt_s[c]
        last = acs_c[q - 1:q, :]
        eacs = jnp.exp(acs_c)
        wend = dt_c * jnp.exp(last - acs_c)
        dec_last = jnp.exp(last)
        for g in range(M_GROUPS):
            bt_g = bt_s[c, g * D_STATE:(g + 1) * D_STATE, :]
            c_g = xc_s[pl.ds(r0, q), D_INNER + M_GROUPS * D_STATE + g * D_STATE:
                       D_INNER + M_GROUPS * D_STATE + (g + 1) * D_STATE].astype(BF16)
            cb = _dot(c_g, bt_g)
            ystate = _dot(c_g, ht_s[g].astype(BF16))
            x_g = xc_s[pl.ds(r0, q), g * GW:(g + 1) * GW]
            ys, xws, decs = [], [], []
            for r in range(M_HPG):
                h = g * M_HPG + r
                x_h = x_g[:, r * M_HEAD_DIM:(r + 1) * M_HEAD_DIM]
                seg = jnp.where(causal, jnp.exp(acs_c[:, h:h + 1] - acst_c[h:h + 1, :]), 0.0)
                m_h = (seg * (cb * dtt_c[h:h + 1, :])).astype(BF16)
                y_h = _dot(m_h, x_h.astype(BF16))
                y_h = y_h + ystate[:, r * M_HEAD_DIM:(r + 1) * M_HEAD_DIM] * eacs[:, h:h + 1]
                ys.append(y_h)
                xws.append(x_h * wend[:, h:h + 1])
                decs.append(jnp.broadcast_to(dec_last[:, h:h + 1], (1, M_HEAD_DIM)))
            y_g = jnp.concatenate(ys, axis=1) + dskip_ref[:, g * GW:(g + 1) * GW] * x_g
            ym_s[pl.ds(r0, q), g * GW:(g + 1) * GW] = y_g
            xw = jnp.concatenate(xws, axis=1).astype(BF16)
            ht_s[g] = ht_s[g] * jnp.concatenate(decs, axis=1) + _dot(bt_g, xw)

        kw = kwin_s[pl.ds(r0, wlen), :]
        vw = vwin_s[pl.ds(r0, wlen), :]
        if not has_state:
            valid = (t0 + r0 - WINDOW + key_j) >= 0
        outs = []
        for j in range(A_KV):
            k_j = kw[:, j * A_HEAD_DIM:(j + 1) * A_HEAD_DIM]
            v_j = vw[:, j * A_HEAD_DIM:(j + 1) * A_HEAD_DIM]
            for r in range(A_REP):
                h = j * A_REP + r
                q_h = q_s[pl.ds(r0, q), h * A_HEAD_DIM:(h + 1) * A_HEAD_DIM]
                s = _dot_nt(q_h, k_j)
                if not has_state:
                    s = jnp.where(valid, s, -jnp.inf)
                sk = sink_ref[h]
                m = jnp.maximum(jnp.max(s, axis=-1, keepdims=True), sk)
                e = jnp.exp(s - m)
                den = jnp.sum(e, axis=-1, keepdims=True) + jnp.exp(sk - m)
                outs.append(_dot(e.astype(BF16), v_j) * (1.0 / den))
        at_s[pl.ds(r0, q), :] = jnp.concatenate(outs, axis=1).astype(BF16)
        return carry

    lax.fori_loop(0, nch, chunk_body, 0)

    if not has_state:
        kwin_s[0:WINDOW, :] = kwin_s[tm:tm + WINDOW, :]
        vwin_s[0:WINDOW, :] = vwin_s[tm:tm + WINDOW, :]

    @pl.when(si == ns - 1)
    def _():
        ssmo_ref[0] = ht_s[...]

    xn = xn_s[...]
    z = _dot(xn, wz_ref[...])
    ym = _rms(ym_s[...] * _silu(z), mng_ref[...]).astype(BF16)
    br_m = _dot(ym, wbm_ref[...])
    br_a = _dot(at_s[...], wba_ref[...])
    g_m = _sigmoid(_dot(xn, wgm_ref[...]))
    g_a = _sigmoid(_dot(xn, wga_ref[...]))
    mixed = (g_m * br_m + g_a * br_a).astype(BF16)
    o = _dot(mixed, wo_ref[...])
    y_ref[...] = x_ref[...] + _rms(o, gpost_ref[...])


def _mixer(x3d, tables, state, wts, *, tm, q, n_keep):
    bsz, seq, _ = x3d.shape
    assert seq % tm == 0 and tm % q == 0 and tm % SUBLANES == 0
    ns = seq // tm
    nch = tm // q
    has_state = state is not None
    if has_state:
        assert ns == 1 and n_keep == tm
    else:
        assert tm >= WINDOW and n_keep == WINDOW and seq >= WINDOW

    in_specs = [pl.BlockSpec((None, tm, D_MODEL), lambda b, s: (b, s, 0))]
    in_specs += [pl.BlockSpec((tm, LANES), lambda b, s: (s, 0))] * 3
    args = [x3d, *tables]
    if has_state:
        conv0, ssm0, ck, cv = state
        in_specs += [
            pl.BlockSpec((1, CONV_W - 1, CONV_DIM), lambda b, s: (b, 0, 0)),
            pl.BlockSpec((1, M_GROUPS, D_STATE, GW), lambda b, s: (b, 0, 0, 0)),
            pl.BlockSpec((1, WINDOW, KV_WIDTH), lambda b, s: (b, 0, 0)),
            pl.BlockSpec((1, WINDOW, KV_WIDTH), lambda b, s: (b, 0, 0)),
        ]
        args += [conv0, ssm0, ck, cv]
    for name, w in wts:
        if name == "sink":
            in_specs.append(pl.BlockSpec(memory_space=pltpu.SMEM))
        else:
            in_specs.append(_const_spec(w.shape))
        args.append(w)

    out_shape = [
        jax.ShapeDtypeStruct((bsz, seq, D_MODEL), F32),
        jax.ShapeDtypeStruct((bsz, CONV_W - 1, CONV_DIM), F32),
        jax.ShapeDtypeStruct((bsz, M_GROUPS, D_STATE, GW), F32),
        jax.ShapeDtypeStruct((bsz, n_keep, KV_WIDTH), F32),
        jax.ShapeDtypeStruct((bsz, n_keep, KV_WIDTH), F32),
    ]
    out_specs = [
        pl.BlockSpec((None, tm, D_MODEL), lambda b, s: (b, s, 0)),
        pl.BlockSpec((1, CONV_W - 1, CONV_DIM), lambda b, s: (b, 0, 0)),
        pl.BlockSpec((1, M_GROUPS, D_STATE, GW), lambda b, s: (b, 0, 0, 0)),
        pl.BlockSpec((1, n_keep, KV_WIDTH), lambda b, s: (b, 0, 0)),
        pl.BlockSpec((1, n_keep, KV_WIDTH), lambda b, s: (b, 0, 0)),
    ]
    scratch = [
        pltpu.VMEM((tm, D_MODEL), BF16),
        pltpu.VMEM((CONV_PAD + tm, CONV_DIM), F32),
        pltpu.VMEM((tm, CONV_DIM), F32),
        pltpu.VMEM((tm, HP), F32),
        pltpu.VMEM((tm, HP), F32),
        pltpu.VMEM((nch, HP, q), F32),
        pltpu.VMEM((nch, HP, q), F32),
        pltpu.VMEM((nch, M_GROUPS * D_STATE, q), BF16),
        pltpu.VMEM((tm, A_WIDTH), BF16),
        pltpu.VMEM((WINDOW + tm, KV_WIDTH), BF16),
        pltpu.VMEM((WINDOW + tm, KV_WIDTH), BF16),
        pltpu.VMEM((M_GROUPS, D_STATE, GW), F32),
        pltpu.VMEM((tm, D_INNER), F32),
        pltpu.VMEM((tm, A_WIDTH), BF16),
    ]
    kern = functools.partial(_mixer_kernel, tm=tm, q=q, n_keep=n_keep, has_state=has_state)
    return pl.pallas_call(
        kern,
        grid=(bsz, ns),
        in_specs=in_specs,
        out_specs=out_specs,
        out_shape=out_shape,
        scratch_shapes=scratch,
        compiler_params=pltpu.CompilerParams(
            dimension_semantics=("arbitrary", "arbitrary"), vmem_limit_bytes=VMEM_LIMIT_BYTES),
        name="mixer_state" if has_state else "mixer_prompt",
    )(*args)


def _rope_tables(pos):
    half = ROT_DIM // 2
    inv_freq = ROPE_THETA ** (-jnp.arange(0, ROT_DIM, 2, dtype=F32) / ROT_DIM)
    ang = pos.astype(F32)[:, None] * inv_freq[None, :]
    cos, sin = jnp.cos(ang), jnp.sin(ang)
    n = pos.shape[0]
    ones = jnp.ones((n, A_HEAD_DIM - ROT_DIM), F32)
    zeros = jnp.zeros((n, A_HEAD_DIM - ROT_DIM), F32)
    zh = jnp.zeros((n, half), F32)
    cos_t = jnp.concatenate([cos, cos, ones], axis=1)
    sa_t = jnp.concatenate([-sin, zh, zeros], axis=1)
    sb_t = jnp.concatenate([zh, sin, zeros], axis=1)
    rep = LANES // A_HEAD_DIM
    return tuple(jnp.tile(t, (1, rep)) for t in (cos_t, sa_t, sb_t))


def _ssm_to_internal(h):
    b = h.shape[0]
    h = h.reshape(b, M_GROUPS, M_HPG, M_HEAD_DIM, D_STATE)
    return jnp.transpose(h, (0, 1, 4, 2, 3)).reshape(b, M_GROUPS, D_STATE, GW)


def _ssm_from_internal(h):
    b = h.shape[0]
    h = h.reshape(b, M_GROUPS, D_STATE, M_HPG, M_HEAD_DIM)
    return jnp.transpose(h, (0, 1, 3, 4, 2)).reshape(b, M_HEADS, M_HEAD_DIM, D_STATE)


def _pick_tile(n, pref):
    t = min(n, pref)
    while n % t:
        t //= 2
    return t


def kernel(x_prompt, x_sample, state_conv, state_ssm, cache_k, cache_v, ffn1_pre_g, ffn1_w_gu, ffn1_w_down, ffn1_post_g, mix_pre_g, w_in, conv_w, conv_b, dt_bias, a_log, d_skip, m_norm_g, attn_sink, w_br_m, w_br_a, w_o, mix_post_g, ffn2_pre_g, ffn2_w_gu, ffn2_w_down, ffn2_post_g):
    depth = w_in.shape[0]
    bp, lp, _ = x_prompt.shape
    bs, ls, _ = x_sample.shape
    pos_p = jnp.arange(lp, dtype=jnp.int32)
    pos_s = PAST_LEN + jnp.arange(ls, dtype=jnp.int32)
    tab_p = _rope_tables(pos_p)
    tab_s = _rope_tables(pos_s)

    yp, ys = x_prompt, x_sample
    new_p, new_s = [], []
    for l in range(depth):
        def row(v):
            return v[l].reshape(1, -1).astype(F32)

        def ffn_w(w_gu, w_down):
            return (w_gu[l][:, :D_FF].astype(BF16), w_gu[l][:, D_FF:].astype(BF16),
                    w_down[l].astype(BF16))

        f1 = ffn_w(ffn1_w_gu, ffn1_w_down)
        f2 = ffn_w(ffn2_w_gu, ffn2_w_down)
        splits = np.cumsum(IN_SIZES)[:-1]
        wz, wxbc, wdt, wq, wk, wv, wgm, wga = jnp.split(w_in[l], splits, axis=1)
        pad_h = HP - M_HEADS
        wdt = jnp.pad(wdt, ((0, 0), (0, pad_h)))
        dtb = jnp.pad(dt_bias[l].astype(F32), (0, pad_h)).reshape(1, HP)
        arow = jnp.pad(-jnp.exp(a_log[l].astype(F32)), (0, pad_h)).reshape(1, HP)
        dskip_row = jnp.repeat(d_skip[l].astype(F32), M_HEAD_DIM).reshape(1, D_INNER)
        wts = [
            ("gpre", row(mix_pre_g)),
            ("wz", wz.astype(BF16)), ("wxbc", wxbc.astype(BF16)), ("wdt", wdt.astype(BF16)),
            ("wq", wq.astype(BF16)), ("wk", wk.astype(BF16)), ("wv", wv.astype(BF16)),
            ("wgm", wgm.astype(BF16)), ("wga", wga.astype(BF16)),
            ("convw", conv_w[l].astype(F32)), ("convb", row(conv_b)),
            ("dtb", dtb), ("arow", arow), ("dskip", dskip_row), ("mng", row(m_norm_g)),
            ("sink", attn_sink[l].astype(F32)),
            ("wbm", w_br_m[l].astype(BF16)), ("wba", w_br_a[l].astype(BF16)),
            ("wo", w_o[l].astype(BF16)), ("gpost", row(mix_post_g)),
        ]

        def layer(x3d, tables, state, tm_ffn, tm_mix, q, n_keep):
            b, s, _ = x3d.shape
            x2d = x3d.reshape(b * s, D_MODEL)
            x2d = _ffn(x2d, row(ffn1_pre_g), *f1, row(ffn1_post_g), _pick_tile(b * s, tm_ffn))
            y3d, conv_o, ssm_o, k_o, v_o = _mixer(
                x2d.reshape(b, s, D_MODEL), tables, state, wts, tm=tm_mix, q=q, n_keep=n_keep)
            y2d = _ffn(y3d.reshape(b * s, D_MODEL), row(ffn2_pre_g), *f2, row(ffn2_post_g),
                       _pick_tile(b * s, tm_ffn))
            return (y2d.reshape(b, s, D_MODEL), conv_o, _ssm_from_internal(ssm_o),
                    k_o.reshape(b, n_keep, A_KV, A_HEAD_DIM), v_o.reshape(b, n_keep, A_KV, A_HEAD_DIM))

        rows_p = min(WINDOW, lp)
        yp, *st_p = layer(yp, tab_p, None, 256, _pick_tile(lp, 256), CHUNK, rows_p)
        n_cache = cache_k.shape[2]
        state = (state_conv[l].astype(F32), _ssm_to_internal(state_ssm[l].astype(F32)),
                 cache_k[l].reshape(bs, n_cache, KV_WIDTH), cache_v[l].reshape(bs, n_cache, KV_WIDTH))
        ys, *st_s = layer(ys, tab_s, state, 256, ls, min(ls, CHUNK), ls)
        new_p.append(st_p)
        new_s.append(st_s)

    conv_p, ssm_p, k_p, v_p = [jnp.stack(t) for t in zip(*new_p)]
    conv_s, ssm_s, k_s, v_s = [jnp.stack(t) for t in zip(*new_s)]
    return (yp, ys, conv_p, ssm_p, k_p, v_p, conv_s, ssm_s, k_s, v_s)
```

```python
import functools

import jax
import jax.numpy as jnp
import numpy as np
from jax import lax
from jax.experimental import pallas as pl
from jax.experimental.pallas import tpu as pltpu

D_MODEL = 1024
CHUNK = 64
EPS = 1e-6
D_INNER = 2 * D_MODEL
M_HEAD_DIM = 64
M_HEADS = D_INNER // M_HEAD_DIM
M_GROUPS = 4
M_HPG = M_HEADS // M_GROUPS
D_STATE = 128
CONV_W = 4
CONV_DIM = D_INNER + 2 * M_GROUPS * D_STATE
A_HEADS = 16
A_KV = 4
A_REP = A_HEADS // A_KV
A_HEAD_DIM = 64
A_WIDTH = A_HEADS * A_HEAD_DIM
KV_WIDTH = A_KV * A_HEAD_DIM
WINDOW = 128
ROT_DIM = A_HEAD_DIM // 4
ROPE_THETA = 500000.0
D_FF = 2816
PAST_LEN = 1024
IN_SIZES = (D_INNER, CONV_DIM, M_HEADS, A_WIDTH, KV_WIDTH, KV_WIDTH, D_MODEL, D_MODEL)

LANES = 128
SUBLANES = 8
BF16_ROWS = 16
VMEM_LIMIT_BYTES = 60 * 1024 * 1024

HP = LANES
GW = M_HPG * M_HEAD_DIM
BC_W = M_GROUPS * D_STATE
PAIR_W = 2 * M_HEAD_DIM
N_PAIRS = M_HEADS // 2
PAIRS_PER_GROUP = M_HPG // 2
CONV_PAD = SUBLANES
DEC_ROWS = BF16_ROWS

BF16 = jnp.bfloat16
F32 = jnp.float32


def _dot(a, b):
    return jnp.dot(a, b, preferred_element_type=F32)


def _dot_nt(a, b):
    return lax.dot_general(a, b, (((1,), (1,)), ((), ())), preferred_element_type=F32)


def _rms(x, g):
    return x * lax.rsqrt(jnp.mean(x * x, axis=-1, keepdims=True) + EPS) * g


def _silu(x):
    return x * (1.0 / (1.0 + jnp.exp(-x)))


def _sigmoid(x):
    return 1.0 / (1.0 + jnp.exp(-x))


def _softplus(x):
    return jnp.maximum(x, 0.0) + jnp.log(1.0 + jnp.exp(-jnp.abs(x)))


def _split3(x):
    hi = x.astype(BF16)
    r = x - hi.astype(F32)
    mid = r.astype(BF16)
    lo = (r - mid.astype(F32)).astype(BF16)
    return hi, mid, lo


def _const_spec(shape):
    zeros = (0,) * len(shape)
    return pl.BlockSpec(shape, lambda *_: zeros, pipeline_mode=pl.Buffered(1))


def _ffn_kernel(x_ref, gpre_ref, wg_ref, wu_ref, wd_ref, gpost_ref, o_ref):
    x = x_ref[...]
    xn = _rms(x, gpre_ref[...]).astype(BF16)
    gate = _dot(xn, wg_ref[...])
    up = _dot(xn, wu_ref[...])
    a = (_silu(gate) * up).astype(BF16)
    y = _dot(a, wd_ref[...])
    o_ref[...] = x + 0.5 * _rms(y, gpost_ref[...])


def _ffn(x2d, gpre, wg, wu, wd, gpost, tm):
    m = x2d.shape[0]
    assert m % tm == 0
    return pl.pallas_call(
        _ffn_kernel,
        grid=(m // tm,),
        in_specs=[
            pl.BlockSpec((tm, D_MODEL), lambda i: (i, 0)),
            _const_spec((1, D_MODEL)),
            _const_spec((D_MODEL, D_FF)),
            _const_spec((D_MODEL, D_FF)),
            _const_spec((D_FF, D_MODEL)),
            _const_spec((1, D_MODEL)),
        ],
        out_specs=pl.BlockSpec((tm, D_MODEL), lambda i: (i, 0)),
        out_shape=jax.ShapeDtypeStruct((m, D_MODEL), F32),
        compiler_params=pltpu.CompilerParams(
            dimension_semantics=("arbitrary",), vmem_limit_bytes=VMEM_LIMIT_BYTES),
        name="ffn_half_step",
    )(x2d, gpre, wg, wu, wd, gpost)


def _rope(x, cos_t, sa_t, sb_t):
    w = x.shape[1]
    reps = w // LANES
    cos_f = jnp.tile(cos_t, (1, reps))
    sa_f = jnp.tile(sa_t, (1, reps))
    sb_f = jnp.tile(sb_t, (1, reps))
    half = ROT_DIM // 2
    return x * cos_f + pltpu.roll(x, w - half, 1) * sa_f + pltpu.roll(x, half, 1) * sb_f


def _transpose_rows_padded(x):
    r, c = x.shape
    rp = -(-r // LANES) * LANES
    cp = -(-c // LANES) * LANES
    if cp != c:
        x = jnp.concatenate([x, jnp.zeros((r, cp - c), x.dtype)], axis=1)
    if rp != r:
        x = jnp.concatenate([x, jnp.zeros((rp - r, cp), x.dtype)], axis=0)
    xt = x.T
    return xt if (rp == r and cp == c) else xt[:c, :r]


def _mixer_kernel(*refs, tm, q, qa, n_keep, has_state):
    it = iter(refs)
    x_ref = next(it)
    cos_ref, sa_ref, sb_ref = next(it), next(it), next(it)
    if has_state:
        conv0_ref, ssm0_ref, ck_ref, cv_ref = next(it), next(it), next(it), next(it)
    (gpre_ref, wz_ref, wxbc_ref, wdt_ref, wq_ref, wk_ref, wv_ref, wgm_ref, wga_ref,
     convw_ref, convb_ref, dtb_ref, arow_ref, dsk_ref, expand_ref, mng_ref, sink_ref,
     wbm_ref, wba_ref, wo_ref, gpost_ref) = [next(it) for _ in range(21)]
    y_ref, convo_ref, ssmo_ref, ko_ref, vo_ref = [next(it) for _ in range(5)]
    (xn_s, cbuf_s, xbd_s, c_s, bt2_s, acs_s, acst2_s, dtt2_s, eax_s, xw_s, decx_s,
     q_s, kwin_s, vwt_s, ht_s, ym_s, at_s) = [next(it) for _ in range(17)]

    si = pl.program_id(1)
    ns = pl.num_programs(1)
    nch = tm // q
    cpb = qa // q
    w2 = 2 * q
    wlen = WINDOW + qa
    assert nch <= DEC_ROWS

    @pl.when(si == 0)
    def _():
        if has_state:
            cbuf_s[CONV_PAD - (CONV_W - 1):CONV_PAD, :] = conv0_ref[0]
            ht_s[...] = ssm0_ref[0]
            kwin_s[0:WINDOW, :] = ck_ref[0].astype(BF16)
            vwt_s[:, 0:WINDOW] = _transpose_rows_padded(cv_ref[0]).astype(BF16)
        else:
            cbuf_s[0:CONV_PAD, :] = jnp.zeros((CONV_PAD, CONV_DIM), F32)
            ht_s[...] = jnp.zeros(ht_s.shape, F32)
            kwin_s[0:WINDOW, :] = jnp.zeros((WINDOW, KV_WIDTH), BF16)
            vwt_s[:, 0:WINDOW] = jnp.zeros((KV_WIDTH, WINDOW), BF16)

    x = x_ref[...]
    xn = _rms(x, gpre_ref[...]).astype(BF16)
    xn_s[...] = xn

    cbuf_s[CONV_PAD:CONV_PAD + tm, :] = _dot(xn, wxbc_ref[...])
    acc = convb_ref[...] + cbuf_s[CONV_PAD:CONV_PAD + tm, :] * convw_ref[CONV_W - 1:CONV_W, :]
    for j in range(CONV_W - 1):
        off = CONV_PAD - (CONV_W - 1) + j
        acc = acc + cbuf_s[off:off + tm, :] * convw_ref[j:j + 1, :]
    xc = _silu(acc)
    tail = cbuf_s[CONV_PAD + tm - (CONV_W - 1):CONV_PAD + tm, :]
    convo_ref[0] = tail
    cbuf_s[CONV_PAD - (CONV_W - 1):CONV_PAD, :] = tail

    xs = xc[:, 0:D_INNER]
    bm = xc[:, D_INNER:D_INNER + BC_W]
    c_s[...] = xc[:, D_INNER + BC_W:].astype(BF16)

    lane_x = lax.broadcasted_iota(jnp.int32, (tm, D_INNER), 1)
    even_head = (lane_x % PAIR_W) < M_HEAD_DIM
    xs_even = jnp.where(even_head, xs, 0.0).astype(BF16)
    xs_odd = jnp.where(even_head, 0.0, xs).astype(BF16)
    for c in range(nch):
        xbd_s[c, 0:q, :] = xs_even[c * q:(c + 1) * q]
        xbd_s[c, q:w2, :] = xs_odd[c * q:(c + 1) * q]

    bdup = jnp.concatenate(
        [bm[c * q:(c + 1) * q] for c in range(nch) for _ in range(2)], axis=0)
    btt = _transpose_rows_padded(bdup)
    for c in range(nch):
        bt2_s[c] = btt[:, c * w2:(c + 1) * w2].astype(BF16)

    dt = _softplus(_dot(xn, wdt_ref[...]) + dtb_ref[...])
    rr = lax.broadcasted_iota(jnp.int32, (tm, tm), 0)
    cc = lax.broadcasted_iota(jnp.int32, (tm, tm), 1)
    tri = jnp.where((rr >= cc) & ((rr // q) == (cc // q)), 1.0, 0.0).astype(BF16)
    cs3 = _dot(tri, jnp.concatenate(_split3(dt * arow_ref[...]), axis=1))
    acs = cs3[:, 0:HP] + cs3[:, HP:2 * HP] + cs3[:, 2 * HP:3 * HP]
    acs_s[...] = acs

    def pair_rows(a):
        nxt = pltpu.roll(a, HP - 1, 1)
        z = jnp.concatenate(
            [p[c * q:(c + 1) * q] for c in range(nch) for p in (a, nxt)], axis=0)
        return _transpose_rows_padded(z)

    acst2 = pair_rows(acs)
    dtt2 = pair_rows(dt)
    for c in range(nch):
        acst2_s[c] = acst2[:, c * w2:(c + 1) * w2]
        dtt2_s[c] = dtt2[:, c * w2:(c + 1) * w2]

    lasts = [acs[c * q + q - 1:c * q + q, :] for c in range(nch)]
    last_b = jnp.concatenate([jnp.broadcast_to(l, (q, HP)) for l in lasts], axis=0)
    eacs = jnp.exp(acs)
    wend = dt * jnp.exp(last_b - acs)
    dec = jnp.exp(jnp.concatenate(lasts + [jnp.zeros((DEC_ROWS - nch, HP), F32)], axis=0))
    lhs = jnp.concatenate([eacs.astype(BF16), wend.astype(BF16), *_split3(dec)], axis=0)
    ex = _dot(lhs, expand_ref[...])
    eax_s[...] = ex[0:tm].astype(BF16)
    xw_s[...] = (xs * ex[tm:2 * tm]).astype(BF16)
    d0 = 2 * tm
    decx_s[...] = (ex[d0:d0 + DEC_ROWS] + ex[d0 + DEC_ROWS:d0 + 2 * DEC_ROWS]
                   + ex[d0 + 2 * DEC_ROWS:d0 + 3 * DEC_ROWS])

    cos_t, sa_t, sb_t = cos_ref[...], sa_ref[...], sb_ref[...]
    qv = _rope(_dot(xn, wq_ref[...]), cos_t, sa_t, sb_t) * (A_HEAD_DIM ** -0.5)
    for h in range(A_HEADS):
        q_s[h] = qv[:, h * A_HEAD_DIM:(h + 1) * A_HEAD_DIM].astype(BF16)
    kv = _rope(_dot(xn, wk_ref[...]), cos_t, sa_t, sb_t)
    vv = _dot(xn, wv_ref[...])
    kwin_s[WINDOW:WINDOW + tm, :] = kv.astype(BF16)
    vwt_s[:, WINDOW:WINDOW + tm] = _transpose_rows_padded(vv).astype(BF16)
    if has_state:
        ko_ref[0] = kv
        vo_ref[0] = vv
    else:
        @pl.when(si == ns - 1)
        def _():
            ko_ref[0] = kv[tm - n_keep:, :]
            vo_ref[0] = vv[tm - n_keep:, :]

    row2 = lax.broadcasted_iota(jnp.int32, (q, w2), 0)
    lane2 = lax.broadcasted_iota(jnp.int32, (q, w2), 1)
    first_half = lane2 < q
    pos2 = jnp.where(first_half, lane2, lane2 - q)
    causal2 = row2 >= pos2
    diag2 = row2 == pos2
    if not has_state:
        krow = lax.broadcasted_iota(jnp.int32, (wlen, qa), 0)
        qcol = lax.broadcasted_iota(jnp.int32, (wlen, qa), 1)
        dchunk = krow // CHUNK - qcol // CHUNK
        band = (dchunk >= 0) & (dchunk <= WINDOW // CHUNK)
    sink_rows = [
        jnp.concatenate([jnp.full((1, qa), sink_ref[j * A_REP + r], F32) for r in range(A_REP)],
                        axis=1) for j in range(A_KV)]

    def ssd_chunk(c):
        rows = slice(c * q, (c + 1) * q)
        acs_c = acs_s[rows, :]
        for g in range(M_GROUPS):
            bt2_g = bt2_s[c, g * D_STATE:(g + 1) * D_STATE, :]
            c_g = c_s[rows, g * D_STATE:(g + 1) * D_STATE]
            cb2 = _dot(c_g, bt2_g)
            ystate = _dot(c_g, ht_s[g].astype(BF16))
            for kk in range(PAIRS_PER_GROUP):
                k = g * PAIRS_PER_GROUP + kk
                h = 2 * k
                lanes = slice(k * PAIR_W, (k + 1) * PAIR_W)
                col2 = jnp.where(first_half, acs_c[:, h:h + 1], acs_c[:, h + 1:h + 2])
                seg = jnp.where(causal2, jnp.exp(col2 - acst2_s[c, h:h + 1, :]), 0.0)
                m2 = seg * (cb2 * dtt2_s[c, h:h + 1, :])
                m2 = m2 + jnp.where(diag2, dsk_ref[k:k + 1, 0:w2], 0.0)
                y2 = _dot(m2.astype(BF16), xbd_s[c, :, lanes])
                y2 = y2 + ystate[:, kk * PAIR_W:(kk + 1) * PAIR_W] * eax_s[rows, lanes].astype(F32)
                ym_s[rows, lanes] = y2.astype(BF16)
            gl = slice(g * GW, (g + 1) * GW)
            ht_s[g] = ht_s[g] * decx_s[c:c + 1, gl] + _dot(bt2_g[:, 0:q], xw_s[rows, gl])

    def attn_block(i):
        ra = i * qa
        rows = slice(ra, ra + qa)
        if not has_state:
            ok = band if ra >= WINDOW else band & ((krow >= WINDOW - ra) | (si > 0))
            bias = jnp.where(ok, 0.0, -jnp.inf)
            bias4 = jnp.concatenate([bias] * A_REP, axis=1)
        for j in range(A_KV):
            hs = slice(j * A_HEAD_DIM, (j + 1) * A_HEAD_DIM)
            qst = q_s[j * A_REP:(j + 1) * A_REP, rows, :].reshape(A_REP * qa, A_HEAD_DIM)
            st = _dot_nt(kwin_s[ra:ra + wlen, hs], qst)
            if not has_state:
                st = st + bias4
            m = jnp.maximum(jnp.max(st, axis=0, keepdims=True), sink_rows[j])
            e = jnp.exp(st - m)
            den = jnp.sum(e, axis=0, keepdims=True) + jnp.exp(sink_rows[j] - m)
            ot = _dot(vwt_s[hs, ra:ra + wlen], e.astype(BF16)) * (1.0 / den)
            for r2 in range(A_REP // 2):
                two = jnp.concatenate(
                    [ot[:, (2 * r2) * qa:(2 * r2 + 1) * qa], ot[:, (2 * r2 + 1) * qa:(2 * r2 + 2) * qa]],
                    axis=0)
                h0 = j * A_REP + 2 * r2
                at_s[rows, h0 * A_HEAD_DIM:(h0 + 2) * A_HEAD_DIM] = (
                    _transpose_rows_padded(two).astype(BF16))

    for i in range(tm // qa):
        for ci in range(cpb):
            ssd_chunk(i * cpb + ci)
        attn_block(i)

    if not has_state:
        kwin_s[0:WINDOW, :] = kwin_s[tm:tm + WINDOW, :]
        vwt_s[:, 0:WINDOW] = vwt_s[:, tm:tm + WINDOW]

    @pl.when(si == ns - 1)
    def _():
        ssmo_ref[0] = ht_s[...]

    xn = xn_s[...]
    z = _dot(xn, wz_ref[...])
    ym = _rms(ym_s[...].astype(F32) * _silu(z), mng_ref[...]).astype(BF16)
    br_m = _dot(ym, wbm_ref[...])
    br_a = _dot(at_s[...], wba_ref[...])
    g_m = _sigmoid(_dot(xn, wgm_ref[...]))
    g_a = _sigmoid(_dot(xn, wga_ref[...]))
    mixed = (g_m * br_m + g_a * br_a).astype(BF16)
    o = _dot(mixed, wo_ref[...])
    y_ref[...] = x_ref[...] + _rms(o, gpost_ref[...])


def _mixer(x3d, tables, state, wts, *, tm, q, qa, n_keep):
    bsz, seq, _ = x3d.shape
    assert seq % tm == 0 and tm % qa == 0 and qa % q == 0 and q % BF16_ROWS == 0
    assert tm >= CONV_W - 1
    ns = seq // tm
    nch = tm // q
    has_state = state is not None
    if has_state:
        assert ns == 1 and n_keep == tm and qa == tm
    else:
        assert tm >= WINDOW and n_keep == WINDOW and seq >= WINDOW and q == CHUNK and qa % CHUNK == 0

    in_specs = [pl.BlockSpec((None, tm, D_MODEL), lambda b, s: (b, s, 0))]
    in_specs += [pl.BlockSpec((tm, LANES), lambda b, s: (s, 0))] * 3
    args = [x3d, *tables]
    if has_state:
        conv0, ssm0, ck, cv = state
        assert ck.shape[1] == WINDOW
        in_specs += [
            pl.BlockSpec((1, CONV_W - 1, CONV_DIM), lambda b, s: (b, 0, 0)),
            pl.BlockSpec((1, M_GROUPS, D_STATE, GW), lambda b, s: (b, 0, 0, 0)),
            pl.BlockSpec((1, WINDOW, KV_WIDTH), lambda b, s: (b, 0, 0)),
            pl.BlockSpec((1, WINDOW, KV_WIDTH), lambda b, s: (b, 0, 0)),
        ]
        args += [conv0, ssm0, ck, cv]
    for name, w in wts:
        if name == "sink":
            in_specs.append(pl.BlockSpec(memory_space=pltpu.SMEM))
        else:
            in_specs.append(_const_spec(w.shape))
        args.append(w)

    out_shape = [
        jax.ShapeDtypeStruct((bsz, seq, D_MODEL), F32),
        jax.ShapeDtypeStruct((bsz, CONV_W - 1, CONV_DIM), F32),
        jax.ShapeDtypeStruct((bsz, M_GROUPS, D_STATE, GW), F32),
        jax.ShapeDtypeStruct((bsz, n_keep, KV_WIDTH), F32),
        jax.ShapeDtypeStruct((bsz, n_keep, KV_WIDTH), F32),
    ]
    out_specs = [
        pl.BlockSpec((None, tm, D_MODEL), lambda b, s: (b, s, 0)),
        pl.BlockSpec((1, CONV_W - 1, CONV_DIM), lambda b, s: (b, 0, 0)),
        pl.BlockSpec((1, M_GROUPS, D_STATE, GW), lambda b, s: (b, 0, 0, 0)),
        pl.BlockSpec((1, n_keep, KV_WIDTH), lambda b, s: (b, 0, 0)),
        pl.BlockSpec((1, n_keep, KV_WIDTH), lambda b, s: (b, 0, 0)),
    ]
    w2 = 2 * q
    scratch = [
        pltpu.VMEM((tm, D_MODEL), BF16),
        pltpu.VMEM((CONV_PAD + tm, CONV_DIM), F32),
        pltpu.VMEM((nch, w2, D_INNER), BF16),
        pltpu.VMEM((tm, BC_W), BF16),
        pltpu.VMEM((nch, BC_W, w2), BF16),
        pltpu.VMEM((tm, HP), F32),
        pltpu.VMEM((nch, HP, w2), F32),
        pltpu.VMEM((nch, HP, w2), F32),
        pltpu.VMEM((tm, D_INNER), BF16),
        pltpu.VMEM((tm, D_INNER), BF16),
        pltpu.VMEM((DEC_ROWS, D_INNER), F32),
        pltpu.VMEM((A_HEADS, tm, A_HEAD_DIM), BF16),
        pltpu.VMEM((WINDOW + tm, KV_WIDTH), BF16),
        pltpu.VMEM((KV_WIDTH, WINDOW + tm), BF16),
        pltpu.VMEM((M_GROUPS, D_STATE, GW), F32),
        pltpu.VMEM((tm, D_INNER), BF16),
        pltpu.VMEM((tm, A_WIDTH), BF16),
    ]
    kern = functools.partial(_mixer_kernel, tm=tm, q=q, qa=qa, n_keep=n_keep, has_state=has_state)
    return pl.pallas_call(
        kern,
        grid=(bsz, ns),
        in_specs=in_specs,
        out_specs=out_specs,
        out_shape=out_shape,
        scratch_shapes=scratch,
        compiler_params=pltpu.CompilerParams(
            dimension_semantics=("arbitrary", "arbitrary"), vmem_limit_bytes=VMEM_LIMIT_BYTES),
        name="mixer_state" if has_state else "mixer_prompt",
    )(*args)


def _rope_tables(pos):
    half = ROT_DIM // 2
    inv_freq = ROPE_THETA ** (-jnp.arange(0, ROT_DIM, 2, dtype=F32) / ROT_DIM)
    ang = pos.astype(F32)[:, None] * inv_freq[None, :]
    cos, sin = jnp.cos(ang), jnp.sin(ang)
    n = pos.shape[0]
    ones = jnp.ones((n, A_HEAD_DIM - ROT_DIM), F32)
    zeros = jnp.zeros((n, A_HEAD_DIM - ROT_DIM), F32)
    zh = jnp.zeros((n, half), F32)
    cos_t = jnp.concatenate([cos, cos, ones], axis=1)
    sa_t = jnp.concatenate([-sin, zh, zeros], axis=1)
    sb_t = jnp.concatenate([zh, sin, zeros], axis=1)
    rep = LANES // A_HEAD_DIM
    return tuple(jnp.tile(t, (1, rep)) for t in (cos_t, sa_t, sb_t))


def _ssm_to_internal(h):
    b = h.shape[0]
    h = h.reshape(b, M_GROUPS, M_HPG, M_HEAD_DIM, D_STATE)
    return jnp.transpose(h, (0, 1, 4, 2, 3)).reshape(b, M_GROUPS, D_STATE, GW)


def _ssm_from_internal(h):
    b = h.shape[0]
    h = h.reshape(b, M_GROUPS, D_STATE, M_HPG, M_HEAD_DIM)
    return jnp.transpose(h, (0, 1, 3, 4, 2)).reshape(b, M_HEADS, M_HEAD_DIM, D_STATE)


def _pick_tile(n, pref):
    t = min(n, pref)
    while n % t:
        t //= 2
    return t


def kernel(x_prompt, x_sample, state_conv, state_ssm, cache_k, cache_v, ffn1_pre_g, ffn1_w_gu, ffn1_w_down, ffn1_post_g, mix_pre_g, w_in, conv_w, conv_b, dt_bias, a_log, d_skip, m_norm_g, attn_sink, w_br_m, w_br_a, w_o, mix_post_g, ffn2_pre_g, ffn2_w_gu, ffn2_w_down, ffn2_post_g):
    depth = w_in.shape[0]
    bp, lp, _ = x_prompt.shape
    bs, ls, _ = x_sample.shape
    pos_p = jnp.arange(lp, dtype=jnp.int32)
    pos_s = PAST_LEN + jnp.arange(ls, dtype=jnp.int32)
    tab_p = _rope_tables(pos_p)
    tab_s = _rope_tables(pos_s)
    expand = (jnp.arange(HP)[:, None] == (jnp.arange(D_INNER)[None, :] // M_HEAD_DIM)).astype(BF16)

    yp, ys = x_prompt, x_sample
    new_p, new_s = [], []
    for l in range(depth):
        def row(v):
            return v[l].reshape(1, -1).astype(F32)

        def ffn_w(w_gu, w_down):
            return (w_gu[l][:, :D_FF].astype(BF16), w_gu[l][:, D_FF:].astype(BF16),
                    w_down[l].astype(BF16))

        f1 = ffn_w(ffn1_w_gu, ffn1_w_down)
        f2 = ffn_w(ffn2_w_gu, ffn2_w_down)
        splits = np.cumsum(IN_SIZES)[:-1]
        wz, wxbc, wdt, wq, wk, wv, wgm, wga = jnp.split(w_in[l], splits, axis=1)
        pad_h = HP - M_HEADS
        wdt = jnp.pad(wdt, ((0, 0), (0, pad_h)))
        dtb = jnp.pad(dt_bias[l].astype(F32), (0, pad_h)).reshape(1, HP)
        arow = jnp.pad(-jnp.exp(a_log[l].astype(F32)), (0, pad_h)).reshape(1, HP)
        dsk = jnp.repeat(d_skip[l].astype(F32).reshape(N_PAIRS, 2), M_HEAD_DIM, axis=1)

        def dsk_for(q):
            d = d_skip[l].astype(F32).reshape(N_PAIRS, 2)
            return jnp.repeat(d, q, axis=1)

        def wts_for(q):
            return [
                ("gpre", row(mix_pre_g)),
                ("wz", wz.astype(BF16)), ("wxbc", wxbc.astype(BF16)), ("wdt", wdt.astype(BF16)),
                ("wq", wq.astype(BF16)), ("wk", wk.astype(BF16)), ("wv", wv.astype(BF16)),
                ("wgm", wgm.astype(BF16)), ("wga", wga.astype(BF16)),
                ("convw", conv_w[l].astype(F32)), ("convb", row(conv_b)),
                ("dtb", dtb), ("arow", arow), ("dsk", dsk_for(q)), ("expand", expand),
                ("mng", row(m_norm_g)), ("sink", attn_sink[l].astype(F32)),
                ("wbm", w_br_m[l].astype(BF16)), ("wba", w_br_a[l].astype(BF16)),
                ("wo", w_o[l].astype(BF16)), ("gpost", row(mix_post_g)),
            ]

        def layer(x3d, tables, state, tm_ffn, tm_mix, q, qa, n_keep):
            b, s, _ = x3d.shape
            x2d = x3d.reshape(b * s, D_MODEL)
            x2d = _ffn(x2d, row(ffn1_pre_g), *f1, row(ffn1_post_g), _pick_tile(b * s, tm_ffn))
            y3d, conv_o, ssm_o, k_o, v_o = _mixer(
                x2d.reshape(b, s, D_MODEL), tables, state, wts_for(q),
                tm=tm_mix, q=q, qa=qa, n_keep=n_keep)
            y2d = _ffn(y3d.reshape(b * s, D_MODEL), row(ffn2_pre_g), *f2, row(ffn2_post_g),
                       _pick_tile(b * s, tm_ffn))
            return (y2d.reshape(b, s, D_MODEL), conv_o, _ssm_from_internal(ssm_o),
                    k_o.reshape(b, n_keep, A_KV, A_HEAD_DIM), v_o.reshape(b, n_keep, A_KV, A_HEAD_DIM))

        rows_p = min(WINDOW, lp)
        yp, *st_p = layer(yp, tab_p, None, 256, _pick_tile(lp, 256), CHUNK, 2 * CHUNK, rows_p)
        n_cache = cache_k.shape[2]
        state = (state_conv[l].astype(F32), _ssm_to_internal(state_ssm[l].astype(F32)),
                 cache_k[l].reshape(bs, n_cache, KV_WIDTH), cache_v[l].reshape(bs, n_cache, KV_WIDTH))
        q_s = min(ls, CHUNK)
        ys, *st_s = layer(ys, tab_s, state, 256, ls, q_s, ls, ls)
        new_p.append(st_p)
        new_s.append(st_s)

    conv_p, ssm_p, k_p, v_p = [jnp.stack(t) for t in zip(*new_p)]
    conv_s, ssm_s, k_s, v_s = [jnp.stack(t) for t in zip(*new_s)]
    return (yp, ys, conv_p, ssm_p, k_p, v_p, conv_s, ssm_s, k_s, v_s)
```

```python
import functools

import jax
import jax.numpy as jnp
import numpy as np
from jax import lax
from jax.experimental import pallas as pl
from jax.experimental.pallas import tpu as pltpu

D_MODEL = 1024
CHUNK = 64
EPS = 1e-6
D_INNER = 2 * D_MODEL
M_HEAD_DIM = 64
M_HEADS = D_INNER // M_HEAD_DIM
M_GROUPS = 4
M_HPG = M_HEADS // M_GROUPS
D_STATE = 128
CONV_W = 4
CONV_DIM = D_INNER + 2 * M_GROUPS * D_STATE
A_HEADS = 16
A_KV = 4
A_REP = A_HEADS // A_KV
A_HEAD_DIM = 64
A_WIDTH = A_HEADS * A_HEAD_DIM
KV_WIDTH = A_KV * A_HEAD_DIM
WINDOW = 128
ROT_DIM = A_HEAD_DIM // 4
ROPE_THETA = 500000.0
D_FF = 2816
PAST_LEN = 1024
IN_SIZES = (D_INNER, CONV_DIM, M_HEADS, A_WIDTH, KV_WIDTH, KV_WIDTH, D_MODEL, D_MODEL)

LANES = 128
SUBLANES = 8
BF16_ROWS = 16
VMEM_LIMIT_BYTES = 60 * 1024 * 1024

HP = LANES
GW = M_HPG * M_HEAD_DIM
BC_W = M_GROUPS * D_STATE
PAIR_W = 2 * M_HEAD_DIM
N_PAIRS = M_HEADS // 2
PAIRS_PER_GROUP = M_HPG // 2
CONV_PAD = SUBLANES
COL_BLK = 512
SIDE_BLK = 256
DEC_ROWS = BF16_ROWS

BF16 = jnp.bfloat16
F32 = jnp.float32


def _dot(a, b):
    return jnp.dot(a, b, preferred_element_type=F32)


def _dot_nt(a, b):
    return lax.dot_general(a, b, (((1,), (1,)), ((), ())), preferred_element_type=F32)


def _rms(x, g):
    return x * lax.rsqrt(jnp.mean(x * x, axis=-1, keepdims=True) + EPS) * g


def _silu(x):
    return x * (1.0 / (1.0 + jnp.exp(-x)))


def _sigmoid(x):
    return 1.0 / (1.0 + jnp.exp(-x))


def _softplus(x):
    return jnp.maximum(x, 0.0) + jnp.log(1.0 + jnp.exp(-jnp.abs(x)))


def _split3(x):
    hi = x.astype(BF16)
    r = x - hi.astype(F32)
    mid = r.astype(BF16)
    lo = (r - mid.astype(F32)).astype(BF16)
    return hi, mid, lo


def _const_spec(shape):
    zeros = (0,) * len(shape)
    return pl.BlockSpec(shape, lambda *_: zeros, pipeline_mode=pl.Buffered(1))


def _ffn_kernel(x_ref, gpre_ref, wg_ref, wu_ref, wd_ref, gpost_ref, o_ref):
    x = x_ref[...]
    xn = _rms(x, gpre_ref[...]).astype(BF16)
    gate = _dot(xn, wg_ref[...])
    up = _dot(xn, wu_ref[...])
    a = (_silu(gate) * up).astype(BF16)
    y = _dot(a, wd_ref[...])
    o_ref[...] = x + 0.5 * _rms(y, gpost_ref[...])


def _ffn(x2d, gpre, wg, wu, wd, gpost, tm):
    m = x2d.shape[0]
    assert m % tm == 0
    return pl.pallas_call(
        _ffn_kernel,
        grid=(m // tm,),
        in_specs=[
            pl.BlockSpec((tm, D_MODEL), lambda i: (i, 0)),
            _const_spec((1, D_MODEL)),
            _const_spec((D_MODEL, D_FF)),
            _const_spec((D_MODEL, D_FF)),
            _const_spec((D_FF, D_MODEL)),
            _const_spec((1, D_MODEL)),
        ],
        out_specs=pl.BlockSpec((tm, D_MODEL), lambda i: (i, 0)),
        out_shape=jax.ShapeDtypeStruct((m, D_MODEL), F32),
        compiler_params=pltpu.CompilerParams(
            dimension_semantics=("arbitrary",), vmem_limit_bytes=VMEM_LIMIT_BYTES),
        name="ffn_half_step",
    )(x2d, gpre, wg, wu, wd, gpost)


def _rope(x, cos_t, sa_t, sb_t):
    w = x.shape[1]
    reps = w // LANES
    cos_f = jnp.tile(cos_t, (1, reps))
    sa_f = jnp.tile(sa_t, (1, reps))
    sb_f = jnp.tile(sb_t, (1, reps))
    half = ROT_DIM // 2
    return x * cos_f + pltpu.roll(x, w - half, 1) * sa_f + pltpu.roll(x, half, 1) * sb_f


def _transpose_rows_padded(x):
    r, c = x.shape
    rp = -(-r // LANES) * LANES
    cp = -(-c // LANES) * LANES
    if cp != c:
        x = jnp.concatenate([x, jnp.zeros((r, cp - c), x.dtype)], axis=1)
    if rp != r:
        x = jnp.concatenate([x, jnp.zeros((rp - r, cp), x.dtype)], axis=0)
    xt = x.T
    return xt if (rp == r and cp == c) else xt[:c, :r]


def _mixer_kernel(*refs, tm, q, qa, n_keep, has_state):
    it = iter(refs)
    x_ref = next(it)
    cos_ref, sa_ref, sb_ref = next(it), next(it), next(it)
    if has_state:
        conv0_ref, ssm0_ref, ck_ref, cv_ref = next(it), next(it), next(it), next(it)
    (gpre_ref, wxbc_ref, wdt_ref, wq_ref, wk_ref, wv_ref,
     convw_ref, convb_ref, dtb_ref, arow_ref, dsk_ref, expand_ref, sink_ref,
     wz_ref, wgm_ref, wga_ref, mng_ref, wbm_ref, wba_ref, wo_ref, gpost_ref) = [
         next(it) for _ in range(21)]
    y_ref, convo_ref, ssmo_ref, ko_ref, vo_ref = [next(it) for _ in range(5)]
    (cbuf_s, xbd_s, c_s, bt_s, acs_s, acst2_s, dtt2_s, eax_s, xw_s, decx_s,
     q_s, kwin_s, vwt_s, ht_s, xn_s, ym_s, at_s, zs_s, gm_s, ga_s) = [next(it) for _ in range(20)]

    si = pl.program_id(1)
    ns = pl.num_programs(1)
    nch = tm // q
    cpb = qa // q
    w2 = 2 * q
    wlen = WINDOW + qa
    assert nch <= DEC_ROWS

    @pl.when(si == 0)
    def _():
        if has_state:
            cbuf_s[CONV_PAD - (CONV_W - 1):CONV_PAD, :] = conv0_ref[0]
            ht_s[...] = ssm0_ref[0]
            kwin_s[0:WINDOW, :] = ck_ref[0].astype(BF16)
            vwt_s[:, 0:WINDOW] = _transpose_rows_padded(cv_ref[0]).astype(BF16)
        else:
            cbuf_s[0:CONV_PAD, :] = jnp.zeros((CONV_PAD, CONV_DIM), F32)
            ht_s[...] = jnp.zeros(ht_s.shape, F32)
            kwin_s[0:WINDOW, :] = jnp.zeros((WINDOW, KV_WIDTH), BF16)
            vwt_s[:, 0:WINDOW] = jnp.zeros((KV_WIDTH, WINDOW), BF16)

    x = x_ref[...]
    xn = _rms(x, gpre_ref[...]).astype(BF16)
    xn_s[...] = xn

    dt = _softplus(_dot(xn, wdt_ref[...]) + dtb_ref[...])
    rr = lax.broadcasted_iota(jnp.int32, (tm, tm), 0)
    cc = lax.broadcasted_iota(jnp.int32, (tm, tm), 1)
    tri = jnp.where((rr >= cc) & ((rr // q) == (cc // q)), 1.0, 0.0).astype(BF16)
    cs3 = _dot(tri, jnp.concatenate(_split3(dt * arow_ref[...]), axis=1))
    acs = cs3[:, 0:HP] + cs3[:, HP:2 * HP] + cs3[:, 2 * HP:3 * HP]
    acs_s[...] = acs
    lasts = [acs[c * q + q - 1:c * q + q, :] for c in range(nch)]
    last_b = jnp.concatenate([jnp.broadcast_to(l, (q, HP)) for l in lasts], axis=0)
    eacs = jnp.exp(acs)
    wend = dt * jnp.exp(last_b - acs)
    dec = jnp.exp(jnp.concatenate(lasts + [jnp.zeros((DEC_ROWS - nch, HP), F32)], axis=0))
    lhs = jnp.concatenate([eacs.astype(BF16), wend.astype(BF16), *_split3(dec)], axis=0)

    def pair_rows(a):
        nxt = pltpu.roll(a, HP - 1, 1)
        z = jnp.concatenate(
            [p[c * q:(c + 1) * q] for c in range(nch) for p in (a, nxt)], axis=0)
        return _transpose_rows_padded(z)

    acst2 = pair_rows(acs)
    dtt2 = pair_rows(dt)
    for c in range(nch):
        acst2_s[c] = acst2[:, c * w2:(c + 1) * w2]
        dtt2_s[c] = dtt2[:, c * w2:(c + 1) * w2]

    lane_b = lax.broadcasted_iota(jnp.int32, (tm, COL_BLK), 1)
    even_head = (lane_b % PAIR_W) < M_HEAD_DIM
    d0 = 2 * tm
    for jb in range(CONV_DIM // COL_BLK):
        cols = slice(jb * COL_BLK, (jb + 1) * COL_BLK)
        cbuf_s[CONV_PAD:CONV_PAD + tm, cols] = _dot(xn, wxbc_ref[:, cols])
        acc = convb_ref[:, cols] + (cbuf_s[CONV_PAD:CONV_PAD + tm, cols]
                                    * convw_ref[CONV_W - 1:CONV_W, cols])
        for j in range(CONV_W - 1):
            off = CONV_PAD - (CONV_W - 1) + j
            acc = acc + cbuf_s[off:off + tm, cols] * convw_ref[j:j + 1, cols]
        xc = _silu(acc)
        if jb < D_INNER // COL_BLK:
            xe = jnp.where(even_head, xc, 0.0).astype(BF16)
            xo = jnp.where(even_head, 0.0, xc).astype(BF16)
            for c in range(nch):
                xbd_s[c, 0:q, cols] = xe[c * q:(c + 1) * q]
                xbd_s[c, q:w2, cols] = xo[c * q:(c + 1) * q]
            ex = _dot(lhs, expand_ref[:, cols])
            eax_s[:, cols] = ex[0:tm].astype(BF16)
            xw_s[:, cols] = (xc * ex[tm:d0]).astype(BF16)
            decx_s[:, cols] = (ex[d0:d0 + DEC_ROWS] + ex[d0 + DEC_ROWS:d0 + 2 * DEC_ROWS]
                               + ex[d0 + 2 * DEC_ROWS:d0 + 3 * DEC_ROWS])
        elif jb == D_INNER // COL_BLK:
            btt = _transpose_rows_padded(xc)
            for c in range(nch):
                bt_s[c] = btt[:, c * q:(c + 1) * q].astype(BF16)
        else:
            c_s[...] = xc.astype(BF16)
    tail = cbuf_s[CONV_PAD + tm - (CONV_W - 1):CONV_PAD + tm, :]
    convo_ref[0] = tail
    cbuf_s[CONV_PAD - (CONV_W - 1):CONV_PAD, :] = tail

    cos_t, sa_t, sb_t = cos_ref[...], sa_ref[...], sb_ref[...]
    qv = _rope(_dot(xn, wq_ref[...]), cos_t, sa_t, sb_t) * (A_HEAD_DIM ** -0.5)
    for h in range(A_HEADS):
        q_s[h] = qv[:, h * A_HEAD_DIM:(h + 1) * A_HEAD_DIM].astype(BF16)
    kv = _rope(_dot(xn, wk_ref[...]), cos_t, sa_t, sb_t)
    vv = _dot(xn, wv_ref[...])
    kwin_s[WINDOW:WINDOW + tm, :] = kv.astype(BF16)
    vwt_s[:, WINDOW:WINDOW + tm] = _transpose_rows_padded(vv).astype(BF16)
    ko_ref[0] = kv[tm - n_keep:, :]
    vo_ref[0] = vv[tm - n_keep:, :]

    row2 = lax.broadcasted_iota(jnp.int32, (q, w2), 0)
    lane2 = lax.broadcasted_iota(jnp.int32, (q, w2), 1)
    first_half = lane2 < q
    pos2 = jnp.where(first_half, lane2, lane2 - q)
    row_g = lax.broadcasted_iota(jnp.int32, (q, PAIRS_PER_GROUP * w2), 0)
    pos_g = lax.broadcasted_iota(jnp.int32, (q, PAIRS_PER_GROUP * w2), 1) % q
    causal_g = row_g >= pos_g
    diag_g = row_g == pos_g
    if not has_state:
        krow = lax.broadcasted_iota(jnp.int32, (wlen, qa), 0)
        qcol = lax.broadcasted_iota(jnp.int32, (wlen, qa), 1)
        dchunk = krow // CHUNK - qcol // CHUNK
        band = (dchunk >= 0) & (dchunk <= WINDOW // CHUNK)
    sink_rows = [
        jnp.concatenate([jnp.full((1, qa), sink_ref[j * A_REP + r], F32) for r in range(A_REP)],
                        axis=1) for j in range(A_KV)]

    def ssd_group(c, g):
        rows = slice(c * q, (c + 1) * q)
        acs_c = acs_s[rows, :]
        if True:
            bt_g = bt_s[c, g * D_STATE:(g + 1) * D_STATE, :]
            c_g = c_s[rows, g * D_STATE:(g + 1) * D_STATE]
            cb = _dot(c_g, bt_g)
            cb2 = jnp.concatenate([cb, cb], axis=1)
            ystate = _dot(c_g, ht_s[g].astype(BF16))
            gl = slice(g * GW, (g + 1) * GW)
            pairs = range(g * PAIRS_PER_GROUP, (g + 1) * PAIRS_PER_GROUP)
            col = jnp.concatenate(
                [jnp.where(first_half, acs_c[:, 2 * k:2 * k + 1], acs_c[:, 2 * k + 1:2 * k + 2])
                 for k in pairs], axis=1)
            arow = jnp.concatenate([acst2_s[c, 2 * k:2 * k + 1, :] for k in pairs], axis=1)
            drow = jnp.concatenate([dtt2_s[c, 2 * k:2 * k + 1, :] for k in pairs], axis=1)
            dskw = jnp.concatenate([dsk_ref[k:k + 1, 0:w2] for k in pairs], axis=1)
            seg = jnp.where(causal_g, jnp.exp(col - arow), 0.0)
            m2 = seg * (jnp.concatenate([cb2] * PAIRS_PER_GROUP, axis=1) * drow)
            m2 = (m2 + jnp.where(diag_g, dskw, 0.0)).astype(BF16)
            y2 = jnp.concatenate(
                [_dot(m2[:, kk * w2:(kk + 1) * w2], xbd_s[c, :, k * PAIR_W:(k + 1) * PAIR_W])
                 for kk, k in enumerate(pairs)], axis=1)
            ym_s[rows, gl] = (y2 + ystate * eax_s[rows, gl].astype(F32)).astype(BF16)
            ht_s[g] = ht_s[g] * decx_s[c:c + 1, gl] + _dot(bt_g, xw_s[rows, gl])

    def attn_head(i, j):
        ra = i * qa
        rows = slice(ra, ra + qa)
        if not has_state:
            ok = band if ra >= WINDOW else band & ((krow >= WINDOW - ra) | (si > 0))
            bias = jnp.where(ok, 0.0, -jnp.inf)
            bias4 = jnp.concatenate([bias] * A_REP, axis=1)
        if True:
            hs = slice(j * A_HEAD_DIM, (j + 1) * A_HEAD_DIM)
            qst = q_s[j * A_REP:(j + 1) * A_REP, rows, :].reshape(A_REP * qa, A_HEAD_DIM)
            st = _dot_nt(kwin_s[ra:ra + wlen, hs], qst)
            if not has_state:
                st = st + bias4
            m = jnp.maximum(jnp.max(st, axis=0, keepdims=True), sink_rows[j])
            e = jnp.exp(st - m)
            den = jnp.sum(e, axis=0, keepdims=True) + jnp.exp(sink_rows[j] - m)
            ot = _dot(vwt_s[hs, ra:ra + wlen], e.astype(BF16)) * (1.0 / den)
            for r2 in range(A_REP // 2):
                two = jnp.concatenate(
                    [ot[:, (2 * r2) * qa:(2 * r2 + 1) * qa], ot[:, (2 * r2 + 1) * qa:(2 * r2 + 2) * qa]],
                    axis=0)
                h0 = j * A_REP + 2 * r2
                at_s[rows, h0 * A_HEAD_DIM:(h0 + 2) * A_HEAD_DIM] = (
                    _transpose_rows_padded(two).astype(BF16))

    def z_block(jb):
        cols = slice(jb * SIDE_BLK, (jb + 1) * SIDE_BLK)
        zs_s[:, cols] = _silu(_dot(xn_s[...], wz_ref[:, cols])).astype(BF16)

    def gate_block(w_ref, dst, jb):
        cols = slice(jb * SIDE_BLK, (jb + 1) * SIDE_BLK)
        dst[:, cols] = _sigmoid(_dot(xn_s[...], w_ref[:, cols])).astype(BF16)

    side_work = [functools.partial(z_block, jb) for jb in range(D_INNER // SIDE_BLK)]
    side_work += [functools.partial(gate_block, wgm_ref, gm_s, jb) for jb in range(D_MODEL // SIDE_BLK)]
    side_work += [functools.partial(gate_block, wga_ref, ga_s, jb) for jb in range(D_MODEL // SIDE_BLK)]

    units = []
    for i in range(tm // qa):
        for ci in range(cpb):
            units += [functools.partial(ssd_group, i * cpb + ci, g) for g in range(M_GROUPS)]
        units += [functools.partial(attn_head, i, j) for j in range(A_KV)]
    done = 0
    for u, unit in enumerate(units):
        unit()
        upto = -(-len(side_work) * (u + 1) // len(units))
        for side in side_work[done:upto]:
            side()
        done = upto

    def out_stage():
        ym = _rms(ym_s[...].astype(F32) * zs_s[...].astype(F32), mng_ref[...]).astype(BF16)
        br_m = _dot(ym, wbm_ref[...])
        br_a = _dot(at_s[...], wba_ref[...])
        mixed = (gm_s[...].astype(F32) * br_m + ga_s[...].astype(F32) * br_a).astype(BF16)
        y_ref[...] = x_ref[...] + _rms(_dot(mixed, wo_ref[...]), gpost_ref[...])
        if not has_state:
            kwin_s[0:WINDOW, :] = kwin_s[tm:tm + WINDOW, :]
            vwt_s[:, 0:WINDOW] = vwt_s[:, tm:tm + WINDOW]

    out_stage()

    @pl.when(si == ns - 1)
    def _():
        ssmo_ref[0] = ht_s[...]


def _mixer(x3d, tables, state, wts, *, tm, q, qa, n_keep):
    bsz, seq, _ = x3d.shape
    assert seq % tm == 0 and tm % qa == 0 and qa % q == 0 and q % BF16_ROWS == 0
    assert tm >= CONV_W - 1
    ns = seq // tm
    nch = tm // q
    has_state = state is not None
    if has_state:
        assert ns == 1 and n_keep == tm and qa == tm
    else:
        assert tm >= WINDOW and n_keep == WINDOW and seq >= WINDOW and q == CHUNK and qa % CHUNK == 0

    in_specs = [pl.BlockSpec((None, tm, D_MODEL), lambda b, s: (b, s, 0))]
    in_specs += [pl.BlockSpec((tm, LANES), lambda b, s: (s, 0))] * 3
    args = [x3d, *tables]
    if has_state:
        conv0, ssm0, ck, cv = state
        assert ck.shape[1] == WINDOW
        in_specs += [
            pl.BlockSpec((1, CONV_W - 1, CONV_DIM), lambda b, s: (b, 0, 0)),
            pl.BlockSpec((1, M_GROUPS, D_STATE, GW), lambda b, s: (b, 0, 0, 0)),
            pl.BlockSpec((1, WINDOW, KV_WIDTH), lambda b, s: (b, 0, 0)),
            pl.BlockSpec((1, WINDOW, KV_WIDTH), lambda b, s: (b, 0, 0)),
        ]
        args += [conv0, ssm0, ck, cv]
    for name, w in wts:
        if name == "sink":
            in_specs.append(pl.BlockSpec(memory_space=pltpu.SMEM))
        else:
            in_specs.append(_const_spec(w.shape))
        args.append(w)

    out_shape = [
        jax.ShapeDtypeStruct((bsz, seq, D_MODEL), F32),
        jax.ShapeDtypeStruct((bsz, CONV_W - 1, CONV_DIM), F32),
        jax.ShapeDtypeStruct((bsz, M_GROUPS, D_STATE, GW), F32),
        jax.ShapeDtypeStruct((bsz, n_keep, KV_WIDTH), F32),
        jax.ShapeDtypeStruct((bsz, n_keep, KV_WIDTH), F32),
    ]
    out_specs = [
        pl.BlockSpec((None, tm, D_MODEL), lambda b, s: (b, s, 0)),
        pl.BlockSpec((1, CONV_W - 1, CONV_DIM), lambda b, s: (b, 0, 0)),
        pl.BlockSpec((1, M_GROUPS, D_STATE, GW), lambda b, s: (b, 0, 0, 0)),
        pl.BlockSpec((1, n_keep, KV_WIDTH), lambda b, s: (b, 0, 0)),
        pl.BlockSpec((1, n_keep, KV_WIDTH), lambda b, s: (b, 0, 0)),
    ]
    w2 = 2 * q
    scratch = [
        pltpu.VMEM((CONV_PAD + tm, CONV_DIM), F32),
        pltpu.VMEM((nch, w2, D_INNER), BF16),
        pltpu.VMEM((tm, BC_W), BF16),
        pltpu.VMEM((nch, BC_W, q), BF16),
        pltpu.VMEM((tm, HP), F32),
        pltpu.VMEM((nch, HP, w2), F32),
        pltpu.VMEM((nch, HP, w2), F32),
        pltpu.VMEM((tm, D_INNER), BF16),
        pltpu.VMEM((tm, D_INNER), BF16),
        pltpu.VMEM((DEC_ROWS, D_INNER), F32),
        pltpu.VMEM((A_HEADS, tm, A_HEAD_DIM), BF16),
        pltpu.VMEM((WINDOW + tm, KV_WIDTH), BF16),
        pltpu.VMEM((KV_WIDTH, WINDOW + tm), BF16),
        pltpu.VMEM((M_GROUPS, D_STATE, GW), F32),
        pltpu.VMEM((tm, D_MODEL), BF16),
        pltpu.VMEM((tm, D_INNER), BF16),
        pltpu.VMEM((tm, A_WIDTH), BF16),
        pltpu.VMEM((tm, D_INNER), BF16),
        pltpu.VMEM((tm, D_MODEL), BF16),
        pltpu.VMEM((tm, D_MODEL), BF16),
    ]
    kern = functools.partial(_mixer_kernel, tm=tm, q=q, qa=qa, n_keep=n_keep, has_state=has_state)
    return pl.pallas_call(
        kern,
        grid=(bsz, ns),
        in_specs=in_specs,
        out_specs=out_specs,
        out_shape=out_shape,
        scratch_shapes=scratch,
        compiler_params=pltpu.CompilerParams(
            dimension_semantics=("arbitrary", "arbitrary"), vmem_limit_bytes=VMEM_LIMIT_BYTES),
        name="mixer_state" if has_state else "mixer_prompt",
    )(*args)


def _rope_tables(pos):
    half = ROT_DIM // 2
    inv_freq = ROPE_THETA ** (-jnp.arange(0, ROT_DIM, 2, dtype=F32) / ROT_DIM)
    ang = pos.astype(F32)[:, None] * inv_freq[None, :]
    cos, sin = jnp.cos(ang), jnp.sin(ang)
    n = pos.shape[0]
    ones = jnp.ones((n, A_HEAD_DIM - ROT_DIM), F32)
    zeros = jnp.zeros((n, A_HEAD_DIM - ROT_DIM), F32)
    zh = jnp.zeros((n, half), F32)
    cos_t = jnp.concatenate([cos, cos, ones], axis=1)
    sa_t = jnp.concatenate([-sin, zh, zeros], axis=1)
    sb_t = jnp.concatenate([zh, sin, zeros], axis=1)
    rep = LANES // A_HEAD_DIM
    return tuple(jnp.tile(t, (1, rep)) for t in (cos_t, sa_t, sb_t))


def _ssm_to_internal(h):
    b = h.shape[0]
    h = h.reshape(b, M_GROUPS, M_HPG, M_HEAD_DIM, D_STATE)
    return jnp.transpose(h, (0, 1, 4, 2, 3)).reshape(b, M_GROUPS, D_STATE, GW)


def _ssm_from_internal(h):
    b = h.shape[0]
    h = h.reshape(b, M_GROUPS, D_STATE, M_HPG, M_HEAD_DIM)
    return jnp.transpose(h, (0, 1, 3, 4, 2)).reshape(b, M_HEADS, M_HEAD_DIM, D_STATE)


def _pick_tile(n, pref):
    t = min(n, pref)
    while n % t:
        t //= 2
    return t


def kernel(x_prompt, x_sample, state_conv, state_ssm, cache_k, cache_v, ffn1_pre_g, ffn1_w_gu, ffn1_w_down, ffn1_post_g, mix_pre_g, w_in, conv_w, conv_b, dt_bias, a_log, d_skip, m_norm_g, attn_sink, w_br_m, w_br_a, w_o, mix_post_g, ffn2_pre_g, ffn2_w_gu, ffn2_w_down, ffn2_post_g):
    depth = w_in.shape[0]
    bp, lp, _ = x_prompt.shape
    bs, ls, _ = x_sample.shape
    pos_p = jnp.arange(lp, dtype=jnp.int32)
    pos_s = PAST_LEN + jnp.arange(ls, dtype=jnp.int32)
    tab_p = _rope_tables(pos_p)
    tab_s = _rope_tables(pos_s)
    expand = (jnp.arange(HP)[:, None] == (jnp.arange(D_INNER)[None, :] // M_HEAD_DIM)).astype(BF16)

    yp, ys = x_prompt, x_sample
    new_p, new_s = [], []
    for l in range(depth):
        def row(v):
            return v[l].reshape(1, -1).astype(F32)

        def ffn_w(w_gu, w_down):
            return (w_gu[l][:, :D_FF].astype(BF16), w_gu[l][:, D_FF:].astype(BF16),
                    w_down[l].astype(BF16))

        f1 = ffn_w(ffn1_w_gu, ffn1_w_down)
        f2 = ffn_w(ffn2_w_gu, ffn2_w_down)
        splits = np.cumsum(IN_SIZES)[:-1]
        wz, wxbc, wdt, wq, wk, wv, wgm, wga = jnp.split(w_in[l], splits, axis=1)
        pad_h = HP - M_HEADS
        wdt = jnp.pad(wdt, ((0, 0), (0, pad_h)))
        dtb = jnp.pad(dt_bias[l].astype(F32), (0, pad_h)).reshape(1, HP)
        arow = jnp.pad(-jnp.exp(a_log[l].astype(F32)), (0, pad_h)).reshape(1, HP)
        def dsk_for(q):
            return jnp.repeat(d_skip[l].astype(F32).reshape(N_PAIRS, 2), q, axis=1)

        def wts_for(q):
            return [
                ("gpre", row(mix_pre_g)),
                ("wxbc", wxbc.astype(BF16)), ("wdt", wdt.astype(BF16)),
                ("wq", wq.astype(BF16)), ("wk", wk.astype(BF16)), ("wv", wv.astype(BF16)),
                ("convw", conv_w[l].astype(F32)), ("convb", row(conv_b)),
                ("dtb", dtb), ("arow", arow), ("dsk", dsk_for(q)), ("expand", expand),
                ("sink", attn_sink[l].astype(F32)),
                ("wz", wz.astype(BF16)), ("wgm", wgm.astype(BF16)), ("wga", wga.astype(BF16)),
                ("mng", row(m_norm_g)), ("wbm", w_br_m[l].astype(BF16)),
                ("wba", w_br_a[l].astype(BF16)), ("wo", w_o[l].astype(BF16)),
                ("gpost", row(mix_post_g)),
            ]

        def layer(x3d, tables, state, tm_ffn, tm_mix, q, qa, n_keep):
            b, s, _ = x3d.shape
            x2d = x3d.reshape(b * s, D_MODEL)
            x2d = _ffn(x2d, row(ffn1_pre_g), *f1, row(ffn1_post_g), _pick_tile(b * s, tm_ffn))
            y3d, conv_o, ssm_o, k_o, v_o = _mixer(
                x2d.reshape(b, s, D_MODEL), tables, state, wts_for(q),
                tm=tm_mix, q=q, qa=qa, n_keep=n_keep)
            y2d = _ffn(y3d.reshape(b * s, D_MODEL), row(ffn2_pre_g), *f2, row(ffn2_post_g),
                       _pick_tile(b * s, tm_ffn))
            return (y2d.reshape(b, s, D_MODEL), conv_o, _ssm_from_internal(ssm_o),
                    k_o.reshape(b, n_keep, A_KV, A_HEAD_DIM), v_o.reshape(b, n_keep, A_KV, A_HEAD_DIM))

        rows_p = min(WINDOW, lp)
        yp, *st_p = layer(yp, tab_p, None, 512, _pick_tile(lp, 256), CHUNK, 2 * CHUNK, rows_p)
        n_cache = cache_k.shape[2]
        state = (state_conv[l].astype(F32), _ssm_to_internal(state_ssm[l].astype(F32)),
                 cache_k[l].reshape(bs, n_cache, KV_WIDTH), cache_v[l].reshape(bs, n_cache, KV_WIDTH))
        q_s = min(ls, CHUNK)
        ys, *st_s = layer(ys, tab_s, state, 256, ls, q_s, ls, ls)
        new_p.append(st_p)
        new_s.append(st_s)

    conv_p, ssm_p, k_p, v_p = [jnp.stack(t) for t in zip(*new_p)]
    conv_s, ssm_s, k_s, v_s = [jnp.stack(t) for t in zip(*new_s)]
    return (yp, ys, conv_p, ssm_p, k_p, v_p, conv_s, ssm_s, k_s, v_s)
```

```python
import functools

import jax
import jax.numpy as jnp
import numpy as np
from jax import lax
from jax.experimental import pallas as pl
from jax.experimental.pallas import tpu as pltpu

D_MODEL = 1024
CHUNK = 64
EPS = 1e-6
D_INNER = 2 * D_MODEL
M_HEAD_DIM = 64
M_HEADS = D_INNER // M_HEAD_DIM
M_GROUPS = 4
M_HPG = M_HEADS // M_GROUPS
D_STATE = 128
CONV_W = 4
CONV_DIM = D_INNER + 2 * M_GROUPS * D_STATE
A_HEADS = 16
A_KV = 4
A_REP = A_HEADS // A_KV
A_HEAD_DIM = 64
A_WIDTH = A_HEADS * A_HEAD_DIM
KV_WIDTH = A_KV * A_HEAD_DIM
WINDOW = 128
ROT_DIM = A_HEAD_DIM // 4
ROPE_THETA = 500000.0
LOG2E = 1.4426950408889634
D_FF = 2816
PAST_LEN = 1024
IN_SIZES = (D_INNER, CONV_DIM, M_HEADS, A_WIDTH, KV_WIDTH, KV_WIDTH, D_MODEL, D_MODEL)

LANES = 128
SUBLANES = 8
BF16_ROWS = 16
VMEM_LIMIT_BYTES = 60 * 1024 * 1024

HP = LANES
GW = M_HPG * M_HEAD_DIM
BC_W = M_GROUPS * D_STATE
PAIR_W = 2 * M_HEAD_DIM
N_PAIRS = M_HEADS // 2
PAIRS_PER_GROUP = M_HPG // 2
CONV_PAD = SUBLANES
COL_BLK = 256
SIDE_BLK = 256
DEC_ROWS = BF16_ROWS

BF16 = jnp.bfloat16
F32 = jnp.float32


def _dot(a, b):
    return jnp.dot(a, b, preferred_element_type=F32)


def _dot_nt(a, b):
    return lax.dot_general(a, b, (((1,), (1,)), ((), ())), preferred_element_type=F32)


def _rms(x, g):
    return x * lax.rsqrt(jnp.mean(x * x, axis=-1, keepdims=True) + EPS) * g


def _silu(x):
    return x * (1.0 / (1.0 + jnp.exp(-x)))


def _sigmoid(x):
    return 1.0 / (1.0 + jnp.exp(-x))


def _softplus(x):
    return jnp.maximum(x, 0.0) + jnp.log(1.0 + jnp.exp(-jnp.abs(x)))


def _split3(x):
    hi = x.astype(BF16)
    r = x - hi.astype(F32)
    mid = r.astype(BF16)
    lo = (r - mid.astype(F32)).astype(BF16)
    return hi, mid, lo


def _const_spec(shape):
    zeros = (0,) * len(shape)
    return pl.BlockSpec(shape, lambda *_: zeros, pipeline_mode=pl.Buffered(1))


def _ffn_kernel(x_ref, gpre_ref, wg_ref, wu_ref, wd_ref, gpost_ref, o_ref):
    x = x_ref[...]
    xn = _rms(x, gpre_ref[...]).astype(BF16)
    gate = _dot(xn, wg_ref[:, 0:D_FF])
    up = _dot(xn, wu_ref[:, 0:D_FF])
    a = (_silu(gate) * up).astype(BF16)
    y = _dot(a, wd_ref[:, 0:D_MODEL])
    o_ref[...] = x + 0.5 * _rms(y, gpost_ref[...])


def _ffn(x2d, gpre, wg, wu, wd, gpost, tm):
    m = x2d.shape[0]
    assert m % tm == 0
    return pl.pallas_call(
        _ffn_kernel,
        grid=(m // tm,),
        in_specs=[
            pl.BlockSpec((tm, D_MODEL), lambda i: (i, 0)),
            _const_spec(gpre.shape), _const_spec(wg.shape), _const_spec(wu.shape),
            _const_spec(wd.shape), _const_spec(gpost.shape),
        ],
        out_specs=pl.BlockSpec((tm, D_MODEL), lambda i: (i, 0)),
        out_shape=jax.ShapeDtypeStruct((m, D_MODEL), F32),
        compiler_params=pltpu.CompilerParams(
            dimension_semantics=("arbitrary",), vmem_limit_bytes=VMEM_LIMIT_BYTES),
        name="ffn_half_step",
    )(x2d, gpre, wg, wu, wd, gpost)


def _rope(x, cos_t, sa_t, sb_t):
    w = x.shape[1]
    reps = w // LANES
    cos_f = jnp.tile(cos_t, (1, reps))
    sa_f = jnp.tile(sa_t, (1, reps))
    sb_f = jnp.tile(sb_t, (1, reps))
    half = ROT_DIM // 2
    return x * cos_f + pltpu.roll(x, w - half, 1) * sa_f + pltpu.roll(x, half, 1) * sb_f


def _transpose_rows_padded(x):
    r, c = x.shape
    rp = -(-r // LANES) * LANES
    cp = -(-c // LANES) * LANES
    if cp != c:
        x = jnp.concatenate([x, jnp.zeros((r, cp - c), x.dtype)], axis=1)
    if rp != r:
        x = jnp.concatenate([x, jnp.zeros((rp - r, cp), x.dtype)], axis=0)
    xt = x.T
    return xt if (rp == r and cp == c) else xt[:c, :r]


def _mixer_kernel(*refs, tm, q, qa, n_keep, has_state):
    it = iter(refs)
    x_ref = next(it)
    cos_ref, sa_ref, sb_ref = next(it), next(it), next(it)
    if has_state:
        conv0_ref, ssm0_ref, ck_ref, cv_ref = next(it), next(it), next(it), next(it)
    (gpre_ref, wxbc_ref, wdt_ref, wq_ref, wk_ref, wv_ref,
     convw_ref, convb_ref, dtb_ref, arow_ref, dsk_ref, expand_ref, sink_ref,
     wz_ref, wgm_ref, wga_ref, mng_ref, wbm_ref, wba_ref, wo_ref, gpost_ref) = [
         next(it) for _ in range(21)]
    y_ref, convo_ref, ssmo_ref, ko_ref, vo_ref = [next(it) for _ in range(5)]
    (cbuf_s, xbd_s, c_s, bt_s, acs_s, acst2_s, dtt2_s, eax_s, xw_s, decx_s,
     q_s, kwin_s, vwt_s, ht_s, xn_s, ym_s, at_s, zs_s, gm_s, ga_s) = [next(it) for _ in range(20)]

    si = pl.program_id(1)
    ns = pl.num_programs(1)
    nch = tm // q
    cpb = qa // q
    w2 = 2 * q
    wlen = WINDOW + qa
    assert nch <= DEC_ROWS

    @pl.when(si == 0)
    def _():
        if has_state:
            cbuf_s[CONV_PAD - (CONV_W - 1):CONV_PAD, :] = conv0_ref[0]
            ht_s[...] = ssm0_ref[0]
            kwin_s[0:WINDOW, :] = ck_ref[0].astype(BF16)
            vwt_s[:, 0:WINDOW] = _transpose_rows_padded(cv_ref[0]).astype(BF16)
        else:
            cbuf_s[0:CONV_PAD, :] = jnp.zeros((CONV_PAD, CONV_DIM), F32)
            ht_s[...] = jnp.zeros(ht_s.shape, F32)
            kwin_s[0:WINDOW, :] = jnp.zeros((WINDOW, KV_WIDTH), BF16)
            vwt_s[:, 0:WINDOW] = jnp.zeros((KV_WIDTH, WINDOW), BF16)

    x = x_ref[...]
    xn = _rms(x, gpre_ref[...]).astype(BF16)
    xn_s[...] = xn

    cos_t, sa_t, sb_t = cos_ref[...], sa_ref[...], sb_ref[...]
    lane_b = lax.broadcasted_iota(jnp.int32, (tm, COL_BLK), 1)
    even_head = (lane_b % PAIR_W) < M_HEAD_DIM
    d0 = 2 * tm
    n_xb = D_INNER // COL_BLK
    n_bb = BC_W // COL_BLK
    st = {}

    def dt_cumsum():
        dt = _softplus(_dot(xn, wdt_ref[...]) + dtb_ref[...])
        rr = lax.broadcasted_iota(jnp.int32, (tm, tm), 0)
        cc = lax.broadcasted_iota(jnp.int32, (tm, tm), 1)
        tri = jnp.where((rr >= cc) & ((rr // q) == (cc // q)), 1.0, 0.0).astype(BF16)
        cs3 = _dot(tri, jnp.concatenate(_split3(dt * arow_ref[...]), axis=1))
        acs = cs3[:, 0:HP] + cs3[:, HP:2 * HP] + cs3[:, 2 * HP:3 * HP]
        acs_s[...] = acs
        st["dt"], st["acs"] = dt, acs

    def dt_factors():
        dt, acs = st["dt"], st["acs"]
        lasts = [acs[c * q + q - 1:c * q + q, :] for c in range(nch)]
        last_b = jnp.concatenate([jnp.broadcast_to(l, (q, HP)) for l in lasts], axis=0)
        eacs = jnp.exp2(acs)
        wend = dt * jnp.exp2(last_b - acs)
        dec = jnp.exp2(jnp.concatenate(lasts + [jnp.zeros((DEC_ROWS - nch, HP), F32)], axis=0))
        st["lhs"] = jnp.concatenate([eacs.astype(BF16), wend.astype(BF16), *_split3(dec)], axis=0)

    def pair_rows(name, dst):
        a = st[name]
        nxt = pltpu.roll(a, HP - 1, 1)
        z = jnp.concatenate(
            [p[c * q:(c + 1) * q] for c in range(nch) for p in (a, nxt)], axis=0)
        zt = _transpose_rows_padded(z)
        for c in range(nch):
            dst[c] = zt[:, c * w2:(c + 1) * w2]

    def proj(jb):
        cols = slice(jb * COL_BLK, (jb + 1) * COL_BLK)
        cbuf_s[CONV_PAD:CONV_PAD + tm, cols] = _dot(xn, wxbc_ref[:, cols])

    def conv(jb):
        cols = slice(jb * COL_BLK, (jb + 1) * COL_BLK)
        acc = convb_ref[:, cols] + (cbuf_s[CONV_PAD:CONV_PAD + tm, cols]
                                    * convw_ref[CONV_W - 1:CONV_W, cols])
        for j in range(CONV_W - 1):
            off = CONV_PAD - (CONV_W - 1) + j
            acc = acc + cbuf_s[off:off + tm, cols] * convw_ref[j:j + 1, cols]
        return _silu(acc)

    def x_block(jb):
        cols = slice(jb * COL_BLK, (jb + 1) * COL_BLK)
        xc = conv(jb)
        xe = jnp.where(even_head, xc, 0.0).astype(BF16)
        xo = jnp.where(even_head, 0.0, xc).astype(BF16)
        for c in range(nch):
            xbd_s[c, 0:q, cols] = xe[c * q:(c + 1) * q]
            xbd_s[c, q:w2, cols] = xo[c * q:(c + 1) * q]
        ex = _dot(st["lhs"], expand_ref[:, cols])
        eax_s[:, cols] = ex[0:tm].astype(BF16)
        xw_s[:, cols] = (xc * ex[tm:d0]).astype(BF16)
        decx_s[:, cols] = (ex[d0:d0 + DEC_ROWS] + ex[d0 + DEC_ROWS:d0 + 2 * DEC_ROWS]
                           + ex[d0 + 2 * DEC_ROWS:d0 + 3 * DEC_ROWS])

    def b_block(jb):
        btt = _transpose_rows_padded(conv(jb))
        r = slice((jb - n_xb) * COL_BLK, (jb - n_xb + 1) * COL_BLK)
        for c in range(nch):
            bt_s[c, r, :] = btt[:, c * q:(c + 1) * q].astype(BF16)

    def c_block(jb):
        r = slice((jb - n_xb - n_bb) * COL_BLK, (jb - n_xb - n_bb + 1) * COL_BLK)
        c_s[:, r] = conv(jb).astype(BF16)

    def q_block(jq):
        cols = slice(jq * COL_BLK, (jq + 1) * COL_BLK)
        qv = _rope(_dot(xn, wq_ref[:, cols]), cos_t, sa_t, sb_t) * (LOG2E * A_HEAD_DIM ** -0.5)
        per = COL_BLK // A_HEAD_DIM
        for hh in range(per):
            q_s[jq * per + hh] = qv[:, hh * A_HEAD_DIM:(hh + 1) * A_HEAD_DIM].astype(BF16)

    def kv_block():
        kv = _rope(_dot(xn, wk_ref[...]), cos_t, sa_t, sb_t)
        vv = _dot(xn, wv_ref[...])
        kwin_s[WINDOW:WINDOW + tm, :] = kv.astype(BF16)
        vwt_s[:, WINDOW:WINDOW + tm] = _transpose_rows_padded(vv).astype(BF16)
        ko_ref[0] = kv[tm - n_keep:, :]
        vo_ref[0] = vv[tm - n_keep:, :]

    P = functools.partial
    bc = list(range(n_xb, n_xb + 2 * n_bb))
    fin = [P(b_block, j) if j < n_xb + n_bb else P(c_block, j) for j in bc]
    light = [dt_cumsum, dt_factors, P(pair_rows, "acs", acst2_s), P(pair_rows, "dt", dtt2_s)]
    light += [P(q_block, jq) for jq in range(A_WIDTH // COL_BLK)] + [kv_block]
    order = [P(proj, bc[0])]
    blocks = bc + list(range(n_xb))
    fins = fin + [P(x_block, j) for j in range(n_xb)]
    for i in range(len(blocks)):
        if i + 1 < len(blocks):
            order.append(P(proj, blocks[i + 1]))
        order.append(fins[i])
        if light:
            order.append(light.pop(0))
    order += light
    for stage in order:
        stage()
    tail = cbuf_s[CONV_PAD + tm - (CONV_W - 1):CONV_PAD + tm, :]
    convo_ref[0] = tail
    cbuf_s[CONV_PAD - (CONV_W - 1):CONV_PAD, :] = tail

    row2 = lax.broadcasted_iota(jnp.int32, (q, w2), 0)
    lane2 = lax.broadcasted_iota(jnp.int32, (q, w2), 1)
    first_half = lane2 < q
    pos2 = jnp.where(first_half, lane2, lane2 - q)
    row_g = lax.broadcasted_iota(jnp.int32, (q, PAIRS_PER_GROUP * w2), 0)
    pos_g = lax.broadcasted_iota(jnp.int32, (q, PAIRS_PER_GROUP * w2), 1) % q
    causal_g = row_g >= pos_g
    diag_g = row_g == pos_g
    if not has_state:
        krow = lax.broadcasted_iota(jnp.int32, (wlen, qa), 0)
        qcol = lax.broadcasted_iota(jnp.int32, (wlen, qa), 1)
        dchunk = krow // CHUNK - qcol // CHUNK
        band = (dchunk >= 0) & (dchunk <= WINDOW // CHUNK)
    sink_rows = [
        jnp.concatenate([jnp.full((1, qa), sink_ref[j * A_REP + r], F32) for r in range(A_REP)],
                        axis=1) for j in range(A_KV)]

    def ssd_group(c, g):
        rows = slice(c * q, (c + 1) * q)
        acs_c = acs_s[rows, :]
        if True:
            bt_g = bt_s[c, g * D_STATE:(g + 1) * D_STATE, :]
            c_g = c_s[rows, g * D_STATE:(g + 1) * D_STATE]
            cb = _dot(c_g, bt_g)
            cb2 = jnp.concatenate([cb, cb], axis=1)
            ystate = _dot(c_g, ht_s[g].astype(BF16))
            gl = slice(g * GW, (g + 1) * GW)
            pairs = range(g * PAIRS_PER_GROUP, (g + 1) * PAIRS_PER_GROUP)
            col = jnp.concatenate(
                [jnp.where(first_half, acs_c[:, 2 * k:2 * k + 1], acs_c[:, 2 * k + 1:2 * k + 2])
                 for k in pairs], axis=1)
            arow = jnp.concatenate([acst2_s[c, 2 * k:2 * k + 1, :] for k in pairs], axis=1)
            drow = jnp.concatenate([dtt2_s[c, 2 * k:2 * k + 1, :] for k in pairs], axis=1)
            dskw = jnp.concatenate([dsk_ref[k:k + 1, 0:w2] for k in pairs], axis=1)
            seg = jnp.where(causal_g, jnp.exp2(col - arow), 0.0)
            m2 = seg * (jnp.concatenate([cb2] * PAIRS_PER_GROUP, axis=1) * drow)
            m2 = (m2 + jnp.where(diag_g, dskw, 0.0)).astype(BF16)
            y2 = jnp.concatenate(
                [_dot(m2[:, kk * w2:(kk + 1) * w2], xbd_s[c, :, k * PAIR_W:(k + 1) * PAIR_W])
                 for kk, k in enumerate(pairs)], axis=1)
            ym_s[rows, gl] = (y2 + ystate * eax_s[rows, gl].astype(F32)).astype(BF16)
            ht_s[g] = ht_s[g] * decx_s[c:c + 1, gl] + _dot(bt_g, xw_s[rows, gl])

    def attn_head(i, j):
        ra = i * qa
        rows = slice(ra, ra + qa)
        if not has_state:
            ok = band if ra >= WINDOW else band & ((krow >= WINDOW - ra) | (si > 0))
            bias = jnp.where(ok, 0.0, -jnp.inf)
            bias4 = jnp.concatenate([bias] * A_REP, axis=1)
        if True:
            hs = slice(j * A_HEAD_DIM, (j + 1) * A_HEAD_DIM)
            qst = q_s[j * A_REP:(j + 1) * A_REP, rows, :].reshape(A_REP * qa, A_HEAD_DIM)
            st = _dot_nt(kwin_s[ra:ra + wlen, hs], qst)
            if not has_state:
                st = st + bias4
            m = jnp.maximum(jnp.max(st, axis=0, keepdims=True), sink_rows[j])
            e = jnp.exp2(st - m)
            den = jnp.sum(e, axis=0, keepdims=True) + jnp.exp2(sink_rows[j] - m)
            ot = _dot(vwt_s[hs, ra:ra + wlen], e.astype(BF16)) * (1.0 / den)
            for r2 in range(A_REP // 2):
                two = jnp.concatenate(
                    [ot[:, (2 * r2) * qa:(2 * r2 + 1) * qa], ot[:, (2 * r2 + 1) * qa:(2 * r2 + 2) * qa]],
                    axis=0)
                h0 = j * A_REP + 2 * r2
                at_s[rows, h0 * A_HEAD_DIM:(h0 + 2) * A_HEAD_DIM] = (
                    _transpose_rows_padded(two).astype(BF16))

    def z_block(jb):
        cols = slice(jb * SIDE_BLK, (jb + 1) * SIDE_BLK)
        zs_s[:, cols] = _silu(_dot(xn_s[...], wz_ref[:, cols])).astype(BF16)

    def gate_block(w_ref, dst, jb):
        cols = slice(jb * SIDE_BLK, (jb + 1) * SIDE_BLK)
        dst[:, cols] = _sigmoid(_dot(xn_s[...], w_ref[:, cols])).astype(BF16)

    side_work = [functools.partial(z_block, jb) for jb in range(D_INNER // SIDE_BLK)]
    side_work += [functools.partial(gate_block, wgm_ref, gm_s, jb) for jb in range(D_MODEL // SIDE_BLK)]
    side_work += [functools.partial(gate_block, wga_ref, ga_s, jb) for jb in range(D_MODEL // SIDE_BLK)]

    scan = [functools.partial(ssd_group, c, g) for c in range(nch) for g in range(M_GROUPS)]
    heads = [functools.partial(attn_head, i, j) for i in range(tm // qa) for j in range(A_KV)]
    units = []
    for k, unit in enumerate(scan):
        units.append(unit)
        units += heads[len(heads) * k // len(scan):len(heads) * (k + 1) // len(scan)]
    done = 0
    for u, unit in enumerate(units):
        unit()
        upto = -(-len(side_work) * (u + 1) // len(units))
        for side in side_work[done:upto]:
            side()
        done = upto

    def out_stage():
        ym = _rms(ym_s[...].astype(F32) * zs_s[...].astype(F32), mng_ref[...]).astype(BF16)
        br_m = _dot(ym, wbm_ref[:, 0:D_MODEL])
        br_a = _dot(at_s[...], wba_ref[:, 0:D_MODEL])
        mixed = (gm_s[...].astype(F32) * br_m + ga_s[...].astype(F32) * br_a).astype(BF16)
        y_ref[...] = x_ref[...] + _rms(_dot(mixed, wo_ref[:, 0:D_MODEL]), gpost_ref[...])
        if not has_state:
            kwin_s[0:WINDOW, :] = kwin_s[tm:tm + WINDOW, :]
            vwt_s[:, 0:WINDOW] = vwt_s[:, tm:tm + WINDOW]

    out_stage()

    @pl.when(si == ns - 1)
    def _():
        ssmo_ref[0] = ht_s[...]


def _mixer(x3d, tables, state, wts, *, tm, q, qa, n_keep):
    bsz, seq, _ = x3d.shape
    assert seq % tm == 0 and tm % qa == 0 and qa % q == 0 and q % BF16_ROWS == 0
    assert tm >= CONV_W - 1
    ns = seq // tm
    nch = tm // q
    has_state = state is not None
    if has_state:
        assert ns == 1 and n_keep == tm and qa == tm
    else:
        assert tm >= WINDOW and n_keep == WINDOW and seq >= WINDOW and q == CHUNK and qa % CHUNK == 0

    in_specs = [pl.BlockSpec((None, tm, D_MODEL), lambda b, s: (b, s, 0))]
    in_specs += [pl.BlockSpec((tm, LANES), lambda b, s: (s, 0))] * 3
    args = [x3d, *tables]
    if has_state:
        conv0, ssm0, ck, cv = state
        assert ck.shape[1] == WINDOW
        in_specs += [
            pl.BlockSpec((1, CONV_W - 1, CONV_DIM), lambda b, s: (b, 0, 0)),
            pl.BlockSpec((1, M_GROUPS, D_STATE, GW), lambda b, s: (b, 0, 0, 0)),
            pl.BlockSpec((1, WINDOW, KV_WIDTH), lambda b, s: (b, 0, 0)),
            pl.BlockSpec((1, WINDOW, KV_WIDTH), lambda b, s: (b, 0, 0)),
        ]
        args += [conv0, ssm0, ck, cv]
    for name, w in wts:
        if name == "sink":
            in_specs.append(pl.BlockSpec(memory_space=pltpu.SMEM))
        else:
            in_specs.append(_const_spec(w.shape))
        args.append(w)

    out_shape = [
        jax.ShapeDtypeStruct((bsz, seq, D_MODEL), F32),
        jax.ShapeDtypeStruct((bsz, CONV_W - 1, CONV_DIM), F32),
        jax.ShapeDtypeStruct((bsz, M_GROUPS, D_STATE, GW), F32),
        jax.ShapeDtypeStruct((bsz, n_keep, KV_WIDTH), F32),
        jax.ShapeDtypeStruct((bsz, n_keep, KV_WIDTH), F32),
    ]
    out_specs = [
        pl.BlockSpec((None, tm, D_MODEL), lambda b, s: (b, s, 0)),
        pl.BlockSpec((1, CONV_W - 1, CONV_DIM), lambda b, s: (b, 0, 0)),
        pl.BlockSpec((1, M_GROUPS, D_STATE, GW), lambda b, s: (b, 0, 0, 0)),
        pl.BlockSpec((1, n_keep, KV_WIDTH), lambda b, s: (b, 0, 0)),
        pl.BlockSpec((1, n_keep, KV_WIDTH), lambda b, s: (b, 0, 0)),
    ]
    w2 = 2 * q
    scratch = [
        pltpu.VMEM((CONV_PAD + tm, CONV_DIM), F32),
        pltpu.VMEM((nch, w2, D_INNER), BF16),
        pltpu.VMEM((tm, BC_W), BF16),
        pltpu.VMEM((nch, BC_W, q), BF16),
        pltpu.VMEM((tm, HP), F32),
        pltpu.VMEM((nch, HP, w2), F32),
        pltpu.VMEM((nch, HP, w2), F32),
        pltpu.VMEM((tm, D_INNER), BF16),
        pltpu.VMEM((tm, D_INNER), BF16),
        pltpu.VMEM((DEC_ROWS, D_INNER), F32),
        pltpu.VMEM((A_HEADS, tm, A_HEAD_DIM), BF16),
        pltpu.VMEM((WINDOW + tm, KV_WIDTH), BF16),
        pltpu.VMEM((KV_WIDTH, WINDOW + tm), BF16),
        pltpu.VMEM((M_GROUPS, D_STATE, GW), F32),
        pltpu.VMEM((tm, D_MODEL), BF16),
        pltpu.VMEM((tm, D_INNER), BF16),
        pltpu.VMEM((tm, A_WIDTH), BF16),
        pltpu.VMEM((tm, D_INNER), BF16),
        pltpu.VMEM((tm, D_MODEL), BF16),
        pltpu.VMEM((tm, D_MODEL), BF16),
    ]
    kern = functools.partial(_mixer_kernel, tm=tm, q=q, qa=qa, n_keep=n_keep, has_state=has_state)
    return pl.pallas_call(
        kern,
        grid=(bsz, ns),
        in_specs=in_specs,
        out_specs=out_specs,
        out_shape=out_shape,
        scratch_shapes=scratch,
        compiler_params=pltpu.CompilerParams(
            dimension_semantics=("arbitrary", "arbitrary"), vmem_limit_bytes=VMEM_LIMIT_BYTES),
        name="mixer_state" if has_state else "mixer_prompt",
    )(*args)


def _rope_tables(pos):
    half = ROT_DIM // 2
    inv_freq = ROPE_THETA ** (-jnp.arange(0, ROT_DIM, 2, dtype=F32) / ROT_DIM)
    ang = pos.astype(F32)[:, None] * inv_freq[None, :]
    cos, sin = jnp.cos(ang), jnp.sin(ang)
    n = pos.shape[0]
    ones = jnp.ones((n, A_HEAD_DIM - ROT_DIM), F32)
    zeros = jnp.zeros((n, A_HEAD_DIM - ROT_DIM), F32)
    zh = jnp.zeros((n, half), F32)
    cos_t = jnp.concatenate([cos, cos, ones], axis=1)
    sa_t = jnp.concatenate([-sin, zh, zeros], axis=1)
    sb_t = jnp.concatenate([zh, sin, zeros], axis=1)
    rep = LANES // A_HEAD_DIM
    return tuple(jnp.tile(t, (1, rep)) for t in (cos_t, sa_t, sb_t))


def _ssm_to_internal(h):
    b = h.shape[0]
    h = h.reshape(b, M_GROUPS, M_HPG, M_HEAD_DIM, D_STATE)
    return jnp.transpose(h, (0, 1, 4, 2, 3)).reshape(b, M_GROUPS, D_STATE, GW)


def _ssm_from_internal(h):
    b = h.shape[0]
    h = h.reshape(b, M_GROUPS, D_STATE, M_HPG, M_HEAD_DIM)
    return jnp.transpose(h, (0, 1, 3, 4, 2)).reshape(b, M_HEADS, M_HEAD_DIM, D_STATE)


def _matmul_weight(w):
    w = w.astype(BF16)
    if (w.shape[1] // LANES) % 4 == 0:
        w = jnp.pad(w, ((0, 0), (0, LANES)))
    return w


def _pick_tile(n, pref):
    t = min(n, pref)
    while n % t:
        t //= 2
    return t


def kernel(x_prompt, x_sample, state_conv, state_ssm, cache_k, cache_v, ffn1_pre_g, ffn1_w_gu, ffn1_w_down, ffn1_post_g, mix_pre_g, w_in, conv_w, conv_b, dt_bias, a_log, d_skip, m_norm_g, attn_sink, w_br_m, w_br_a, w_o, mix_post_g, ffn2_pre_g, ffn2_w_gu, ffn2_w_down, ffn2_post_g):
    depth = w_in.shape[0]
    bp, lp, _ = x_prompt.shape
    bs, ls, _ = x_sample.shape
    pos_p = jnp.arange(lp, dtype=jnp.int32)
    pos_s = PAST_LEN + jnp.arange(ls, dtype=jnp.int32)
    tab_p = _rope_tables(pos_p)
    tab_s = _rope_tables(pos_s)
    expand = (jnp.arange(HP)[:, None] == (jnp.arange(D_INNER)[None, :] // M_HEAD_DIM)).astype(BF16)

    yp, ys = x_prompt, x_sample
    new_p, new_s = [], []
    for l in range(depth):
        def row(v):
            return v[l].reshape(1, -1).astype(F32)

        def ffn_w(w_gu, w_down):
            return (_matmul_weight(w_gu[l][:, :D_FF]), _matmul_weight(w_gu[l][:, D_FF:]),
                    _matmul_weight(w_down[l]))

        f1 = ffn_w(ffn1_w_gu, ffn1_w_down)
        f2 = ffn_w(ffn2_w_gu, ffn2_w_down)
        splits = np.cumsum(IN_SIZES)[:-1]
        wz, wxbc, wdt, wq, wk, wv, wgm, wga = jnp.split(w_in[l], splits, axis=1)
        pad_h = HP - M_HEADS
        wdt = jnp.pad(wdt, ((0, 0), (0, pad_h)))
        dtb = jnp.pad(dt_bias[l].astype(F32), (0, pad_h)).reshape(1, HP)
        arow = jnp.pad(-jnp.exp(a_log[l].astype(F32)) * LOG2E, (0, pad_h)).reshape(1, HP)
        def dsk_for(q):
            return jnp.repeat(d_skip[l].astype(F32).reshape(N_PAIRS, 2), q, axis=1)

        def wts_for(q):
            return [
                ("gpre", row(mix_pre_g)),
                ("wxbc", _matmul_weight(wxbc)), ("wdt", wdt.astype(BF16)),
                ("wq", _matmul_weight(wq)), ("wk", wk.astype(BF16)), ("wv", wv.astype(BF16)),
                ("convw", conv_w[l].astype(F32)), ("convb", row(conv_b)),
                ("dtb", dtb), ("arow", arow), ("dsk", dsk_for(q)), ("expand", _matmul_weight(expand)),
                ("sink", attn_sink[l].astype(F32) * LOG2E),
                ("wz", _matmul_weight(wz)), ("wgm", _matmul_weight(wgm)), ("wga", _matmul_weight(wga)),
                ("mng", row(m_norm_g)), ("wbm", _matmul_weight(w_br_m[l])),
                ("wba", _matmul_weight(w_br_a[l])), ("wo", _matmul_weight(w_o[l])),
                ("gpost", row(mix_post_g)),
            ]

        def layer(x3d, tables, state, tm_ffn, tm_mix, q, qa, n_keep):
            b, s, _ = x3d.shape
            x2d = x3d.reshape(b * s, D_MODEL)
            x2d = _ffn(x2d, row(ffn1_pre_g), *f1, row(ffn1_post_g), _pick_tile(b * s, tm_ffn))
            y3d, conv_o, ssm_o, k_o, v_o = _mixer(
                x2d.reshape(b, s, D_MODEL), tables, state, wts_for(q),
                tm=tm_mix, q=q, qa=qa, n_keep=n_keep)
            y2d = _ffn(y3d.reshape(b * s, D_MODEL), row(ffn2_pre_g), *f2, row(ffn2_post_g),
                       _pick_tile(b * s, tm_ffn))
            return (y2d.reshape(b, s, D_MODEL), conv_o, _ssm_from_internal(ssm_o),
                    k_o.reshape(b, n_keep, A_KV, A_HEAD_DIM), v_o.reshape(b, n_keep, A_KV, A_HEAD_DIM))

        rows_p = min(WINDOW, lp)
        yp, *st_p = layer(yp, tab_p, None, 512, _pick_tile(lp, 256), CHUNK, 2 * CHUNK, rows_p)
        n_cache = cache_k.shape[2]
        state = (state_conv[l].astype(F32), _ssm_to_internal(state_ssm[l].astype(F32)),
                 cache_k[l].reshape(bs, n_cache, KV_WIDTH), cache_v[l].reshape(bs, n_cache, KV_WIDTH))
        q_s = min(ls, CHUNK)
        ys, *st_s = layer(ys, tab_s, state, 256, ls, q_s, ls, ls)
        new_p.append(st_p)
        new_s.append(st_s)

    conv_p, ssm_p, k_p, v_p = [jnp.stack(t) for t in zip(*new_p)]
    conv_s, ssm_s, k_s, v_s = [jnp.stack(t) for t in zip(*new_s)]
    return (yp, ys, conv_p, ssm_p, k_p, v_p, conv_s, ssm_s, k_s, v_s)
```

```python
import functools

import jax
import jax.numpy as jnp
import numpy as np
from jax import lax
from jax.experimental import pallas as pl
from jax.experimental.pallas import tpu as pltpu

D_MODEL = 1024
CHUNK = 64
EPS = 1e-6
D_INNER = 2 * D_MODEL
M_HEAD_DIM = 64
M_HEADS = D_INNER // M_HEAD_DIM
M_GROUPS = 4
M_HPG = M_HEADS // M_GROUPS
D_STATE = 128
CONV_W = 4
CONV_DIM = D_INNER + 2 * M_GROUPS * D_STATE
A_HEADS = 16
A_KV = 4
A_REP = A_HEADS // A_KV
A_HEAD_DIM = 64
A_WIDTH = A_HEADS * A_HEAD_DIM
KV_WIDTH = A_KV * A_HEAD_DIM
WINDOW = 128
ROT_DIM = A_HEAD_DIM // 4
ROPE_THETA = 500000.0
LOG2E = 1.4426950408889634
D_FF = 2816
PAST_LEN = 1024
IN_SIZES = (D_INNER, CONV_DIM, M_HEADS, A_WIDTH, KV_WIDTH, KV_WIDTH, D_MODEL, D_MODEL)

LANES = 128
SUBLANES = 8
BF16_ROWS = 16
VMEM_LIMIT_BYTES = 60 * 1024 * 1024

HP = LANES
GW = M_HPG * M_HEAD_DIM
BC_W = M_GROUPS * D_STATE
PAIR_W = 2 * M_HEAD_DIM
N_PAIRS = M_HEADS // 2
PAIRS_PER_GROUP = M_HPG // 2
CONV_PAD = SUBLANES
COL_BLK = 256
SIDE_BLK = 256
FFN_PART_ROWS = 256
DEC_ROWS = BF16_ROWS

BF16 = jnp.bfloat16
F32 = jnp.float32


def _dot(a, b):
    return jnp.dot(a, b, preferred_element_type=F32)


def _dot_nt(a, b):
    return lax.dot_general(a, b, (((1,), (1,)), ((), ())), preferred_element_type=F32)


def _rms(x, g):
    return x * lax.rsqrt(jnp.mean(x * x, axis=-1, keepdims=True) + EPS) * g


def _silu(x):
    return x * (1.0 / (1.0 + jnp.exp(-x)))


def _sigmoid(x):
    return 1.0 / (1.0 + jnp.exp(-x))


def _softplus(x):
    return jnp.maximum(x, 0.0) + jnp.log(1.0 + jnp.exp(-jnp.abs(x)))


def _split3(x):
    hi = x.astype(BF16)
    r = x - hi.astype(F32)
    mid = r.astype(BF16)
    lo = (r - mid.astype(F32)).astype(BF16)
    return hi, mid, lo


def _const_spec(shape):
    zeros = (0,) * len(shape)
    return pl.BlockSpec(shape, lambda *_: zeros, pipeline_mode=pl.Buffered(1))


def _ffn_kernel(x_ref, gpre_ref, wg_ref, wu_ref, wd_ref, gpost_ref, o_ref, *, parts):
    rows = x_ref.shape[0] // parts
    sl = [slice(p * rows, (p + 1) * rows) for p in range(parts)]
    xn = [_rms(x_ref[s, :], gpre_ref[...]).astype(BF16) for s in sl]
    act = []
    for p in range(parts):
        gate = _dot(xn[p], wg_ref[:, 0:D_FF])
        up = _dot(xn[p], wu_ref[:, 0:D_FF])
        act.append((_silu(gate) * up).astype(BF16))
    for p in range(parts):
        y = _dot(act[p], wd_ref[:, 0:D_MODEL])
        o_ref[sl[p], :] = x_ref[sl[p], :] + 0.5 * _rms(y, gpost_ref[...])


def _ffn(x2d, gpre, wg, wu, wd, gpost, tm):
    m = x2d.shape[0]
    assert m % tm == 0
    parts = max(1, tm // FFN_PART_ROWS)
    return pl.pallas_call(
        functools.partial(_ffn_kernel, parts=parts),
        grid=(m // tm,),
        in_specs=[
            pl.BlockSpec((tm, D_MODEL), lambda i: (i, 0)),
            _const_spec(gpre.shape), _const_spec(wg.shape), _const_spec(wu.shape),
            _const_spec(wd.shape), _const_spec(gpost.shape),
        ],
        out_specs=pl.BlockSpec((tm, D_MODEL), lambda i: (i, 0)),
        out_shape=jax.ShapeDtypeStruct((m, D_MODEL), F32),
        compiler_params=pltpu.CompilerParams(
            dimension_semantics=("arbitrary",), vmem_limit_bytes=VMEM_LIMIT_BYTES),
        name="ffn_half_step",
    )(x2d, gpre, wg, wu, wd, gpost)


def _rope(x, cos_t, sa_t, sb_t):
    w = x.shape[1]
    reps = w // LANES
    cos_f = jnp.tile(cos_t, (1, reps))
    sa_f = jnp.tile(sa_t, (1, reps))
    sb_f = jnp.tile(sb_t, (1, reps))
    half = ROT_DIM // 2
    return x * cos_f + pltpu.roll(x, w - half, 1) * sa_f + pltpu.roll(x, half, 1) * sb_f


def _transpose_rows_padded(x):
    r, c = x.shape
    rp = -(-r // LANES) * LANES
    cp = -(-c // LANES) * LANES
    if cp != c:
        x = jnp.concatenate([x, jnp.zeros((r, cp - c), x.dtype)], axis=1)
    if rp != r:
        x = jnp.concatenate([x, jnp.zeros((rp - r, cp), x.dtype)], axis=0)
    xt = x.T
    return xt if (rp == r and cp == c) else xt[:c, :r]


def _mixer_kernel(*refs, tm, q, qa, n_keep, has_state):
    it = iter(refs)
    x_ref = next(it)
    cos_ref, sa_ref, sb_ref = next(it), next(it), next(it)
    if has_state:
        conv0_ref, ssm0_ref, ck_ref, cv_ref = next(it), next(it), next(it), next(it)
    (gpre_ref, wxbc_ref, wdt_ref, wq_ref, wk_ref, wv_ref,
     convw_ref, convb_ref, dtb_ref, arow_ref, dsk_ref, expand_ref, sink_ref,
     wz_ref, wgm_ref, wga_ref, mng_ref, wbm_ref, wba_ref, wo_ref, gpost_ref) = [
         next(it) for _ in range(21)]
    y_ref, convo_ref, ssmo_ref, ko_ref, vo_ref = [next(it) for _ in range(5)]
    (cbuf_s, xbd_s, c_s, bt_s, acs_s, acst2_s, dtt2_s, eax_s, xw_s, decx_s,
     q_s, kwin_s, vwt_s, ht_s, xn_s, ym_s, at_s, zs_s, gm_s, ga_s) = [next(it) for _ in range(20)]

    si = pl.program_id(1)
    ns = pl.num_programs(1)
    nch = tm // q
    cpb = qa // q
    w2 = 2 * q
    wlen = WINDOW + qa
    assert nch <= DEC_ROWS

    @pl.when(si == 0)
    def _():
        if has_state:
            cbuf_s[CONV_PAD - (CONV_W - 1):CONV_PAD, :] = conv0_ref[0]
            for h in range(M_HEADS):
                g, r = divmod(h, M_HPG)
                ht_s[g, :, r * M_HEAD_DIM:(r + 1) * M_HEAD_DIM] = _transpose_rows_padded(ssm0_ref[0, h])
            kwin_s[0:WINDOW, :] = ck_ref[0].astype(BF16)
            vwt_s[:, 0:WINDOW] = _transpose_rows_padded(cv_ref[0]).astype(BF16)
        else:
            cbuf_s[0:CONV_PAD, :] = jnp.zeros((CONV_PAD, CONV_DIM), F32)
            ht_s[...] = jnp.zeros(ht_s.shape, F32)
            kwin_s[0:WINDOW, :] = jnp.zeros((WINDOW, KV_WIDTH), BF16)
            vwt_s[:, 0:WINDOW] = jnp.zeros((KV_WIDTH, WINDOW), BF16)

    x = x_ref[...]
    xn = _rms(x, gpre_ref[...]).astype(BF16)
    xn_s[...] = xn

    cos_t, sa_t, sb_t = cos_ref[...], sa_ref[...], sb_ref[...]
    lane_b = lax.broadcasted_iota(jnp.int32, (tm, COL_BLK), 1)
    even_head = (lane_b % PAIR_W) < M_HEAD_DIM
    d0 = 2 * tm
    n_xb = D_INNER // COL_BLK
    n_bb = BC_W // COL_BLK
    st = {}

    def dt_cumsum():
        dt = _softplus(_dot(xn, wdt_ref[...]) + dtb_ref[...])
        rr = lax.broadcasted_iota(jnp.int32, (tm, tm), 0)
        cc = lax.broadcasted_iota(jnp.int32, (tm, tm), 1)
        tri = jnp.where((rr >= cc) & ((rr // q) == (cc // q)), 1.0, 0.0).astype(BF16)
        cs3 = _dot(tri, jnp.concatenate(_split3(dt * arow_ref[...]), axis=1))
        acs = cs3[:, 0:HP] + cs3[:, HP:2 * HP] + cs3[:, 2 * HP:3 * HP]
        acs_s[...] = acs
        st["dt"], st["acs"] = dt, acs

    def dt_factors():
        dt, acs = st["dt"], st["acs"]
        lasts = [acs[c * q + q - 1:c * q + q, :] for c in range(nch)]
        last_b = jnp.concatenate([jnp.broadcast_to(l, (q, HP)) for l in lasts], axis=0)
        eacs = jnp.exp2(acs)
        wend = dt * jnp.exp2(last_b - acs)
        dec = jnp.exp2(jnp.concatenate(lasts + [jnp.zeros((DEC_ROWS - nch, HP), F32)], axis=0))
        st["lhs"] = jnp.concatenate([eacs.astype(BF16), wend.astype(BF16), *_split3(dec)], axis=0)

    def pair_rows(name, dst):
        a = st[name]
        nxt = pltpu.roll(a, HP - 1, 1)
        z = jnp.concatenate(
            [p[c * q:(c + 1) * q] for c in range(nch) for p in (a, nxt)], axis=0)
        zt = _transpose_rows_padded(z)
        for c in range(nch):
            dst[c] = zt[:, c * w2:(c + 1) * w2]

    def proj(jb):
        cols = slice(jb * COL_BLK, (jb + 1) * COL_BLK)
        cbuf_s[CONV_PAD:CONV_PAD + tm, cols] = _dot(xn, wxbc_ref[:, cols])

    def conv(jb):
        cols = slice(jb * COL_BLK, (jb + 1) * COL_BLK)
        acc = convb_ref[:, cols] + (cbuf_s[CONV_PAD:CONV_PAD + tm, cols]
                                    * convw_ref[CONV_W - 1:CONV_W, cols])
        for j in range(CONV_W - 1):
            off = CONV_PAD - (CONV_W - 1) + j
            acc = acc + cbuf_s[off:off + tm, cols] * convw_ref[j:j + 1, cols]
        return _silu(acc)

    def x_block(jb):
        cols = slice(jb * COL_BLK, (jb + 1) * COL_BLK)
        xc = conv(jb)
        xe = jnp.where(even_head, xc, 0.0).astype(BF16)
        xo = jnp.where(even_head, 0.0, xc).astype(BF16)
        for c in range(nch):
            xbd_s[c, 0:q, cols] = xe[c * q:(c + 1) * q]
            xbd_s[c, q:w2, cols] = xo[c * q:(c + 1) * q]
        ex = _dot(st["lhs"], expand_ref[:, cols])
        eax_s[:, cols] = ex[0:tm].astype(BF16)
        xw_s[:, cols] = (xc * ex[tm:d0]).astype(BF16)
        decx_s[:, cols] = (ex[d0:d0 + DEC_ROWS] + ex[d0 + DEC_ROWS:d0 + 2 * DEC_ROWS]
                           + ex[d0 + 2 * DEC_ROWS:d0 + 3 * DEC_ROWS])

    def b_block(jb):
        btt = _transpose_rows_padded(conv(jb))
        r = slice((jb - n_xb) * COL_BLK, (jb - n_xb + 1) * COL_BLK)
        for c in range(nch):
            bt_s[c, r, :] = btt[:, c * q:(c + 1) * q].astype(BF16)

    def c_block(jb):
        r = slice((jb - n_xb - n_bb) * COL_BLK, (jb - n_xb - n_bb + 1) * COL_BLK)
        c_s[:, r] = conv(jb).astype(BF16)

    def q_block(jq):
        cols = slice(jq * COL_BLK, (jq + 1) * COL_BLK)
        qv = _rope(_dot(xn, wq_ref[:, cols]), cos_t, sa_t, sb_t) * (LOG2E * A_HEAD_DIM ** -0.5)
        per = COL_BLK // A_HEAD_DIM
        for hh in range(per):
            q_s[jq * per + hh] = qv[:, hh * A_HEAD_DIM:(hh + 1) * A_HEAD_DIM].astype(BF16)

    def kv_block():
        kv = _rope(_dot(xn, wk_ref[...]), cos_t, sa_t, sb_t)
        vv = _dot(xn, wv_ref[...])
        kwin_s[WINDOW:WINDOW + tm, :] = kv.astype(BF16)
        vwt_s[:, WINDOW:WINDOW + tm] = _transpose_rows_padded(vv).astype(BF16)
        ko_ref[0] = kv[tm - n_keep:, :]
        vo_ref[0] = vv[tm - n_keep:, :]

    P = functools.partial
    bc = list(range(n_xb, n_xb + 2 * n_bb))
    fin = [P(b_block, j) if j < n_xb + n_bb else P(c_block, j) for j in bc]
    light = [dt_cumsum, dt_factors, P(pair_rows, "acs", acst2_s), P(pair_rows, "dt", dtt2_s)]
    light += [P(q_block, jq) for jq in range(A_WIDTH // COL_BLK)] + [kv_block]
    order = [P(proj, bc[0])]
    blocks = bc + list(range(n_xb))
    fins = fin + [P(x_block, j) for j in range(n_xb)]
    for i in range(len(blocks)):
        if i + 1 < len(blocks):
            order.append(P(proj, blocks[i + 1]))
        order.append(fins[i])
        if light:
            order.append(light.pop(0))
    order += light
    for stage in order:
        stage()
    tail = cbuf_s[CONV_PAD + tm - (CONV_W - 1):CONV_PAD + tm, :]
    convo_ref[0] = tail
    cbuf_s[CONV_PAD - (CONV_W - 1):CONV_PAD, :] = tail

    row2 = lax.broadcasted_iota(jnp.int32, (q, w2), 0)
    lane2 = lax.broadcasted_iota(jnp.int32, (q, w2), 1)
    first_half = lane2 < q
    pos2 = jnp.where(first_half, lane2, lane2 - q)
    row_g = lax.broadcasted_iota(jnp.int32, (q, PAIRS_PER_GROUP * w2), 0)
    pos_g = lax.broadcasted_iota(jnp.int32, (q, PAIRS_PER_GROUP * w2), 1) % q
    causal_g = row_g >= pos_g
    diag_g = row_g == pos_g
    if not has_state:
        krow = lax.broadcasted_iota(jnp.int32, (wlen, qa), 0)
        qcol = lax.broadcasted_iota(jnp.int32, (wlen, qa), 1)
        dchunk = krow // CHUNK - qcol // CHUNK
        band = (dchunk >= 0) & (dchunk <= WINDOW // CHUNK)
    sink_rows = [
        jnp.concatenate([jnp.full((1, qa), sink_ref[j * A_REP + r], F32) for r in range(A_REP)],
                        axis=1) for j in range(A_KV)]

    def ssd_group(c, g):
        rows = slice(c * q, (c + 1) * q)
        acs_c = acs_s[rows, :]
        if True:
            bt_g = bt_s[c, g * D_STATE:(g + 1) * D_STATE, :]
            c_g = c_s[rows, g * D_STATE:(g + 1) * D_STATE]
            cb = _dot(c_g, bt_g)
            cb2 = jnp.concatenate([cb, cb], axis=1)
            ystate = _dot(c_g, ht_s[g].astype(BF16))
            gl = slice(g * GW, (g + 1) * GW)
            pairs = range(g * PAIRS_PER_GROUP, (g + 1) * PAIRS_PER_GROUP)
            col = jnp.concatenate(
                [jnp.where(first_half, acs_c[:, 2 * k:2 * k + 1], acs_c[:, 2 * k + 1:2 * k + 2])
                 for k in pairs], axis=1)
            arow = jnp.concatenate([acst2_s[c, 2 * k:2 * k + 1, :] for k in pairs], axis=1)
            drow = jnp.concatenate([dtt2_s[c, 2 * k:2 * k + 1, :] for k in pairs], axis=1)
            dskw = jnp.concatenate([dsk_ref[k:k + 1, 0:w2] for k in pairs], axis=1)
            seg = jnp.where(causal_g, jnp.exp2(col - arow), 0.0)
            m2 = seg * (jnp.concatenate([cb2] * PAIRS_PER_GROUP, axis=1) * drow)
            m2 = (m2 + jnp.where(diag_g, dskw, 0.0)).astype(BF16)
            y2 = jnp.concatenate(
                [_dot(m2[:, kk * w2:(kk + 1) * w2], xbd_s[c, :, k * PAIR_W:(k + 1) * PAIR_W])
                 for kk, k in enumerate(pairs)], axis=1)
            ym_s[rows, gl] = (y2 + ystate * eax_s[rows, gl].astype(F32)).astype(BF16)
            ht_s[g] = ht_s[g] * decx_s[c:c + 1, gl] + _dot(bt_g, xw_s[rows, gl])

    def attn_head(i, j):
        ra = i * qa
        rows = slice(ra, ra + qa)
        if not has_state:
            ok = band if ra >= WINDOW else band & ((krow >= WINDOW - ra) | (si > 0))
            bias = jnp.where(ok, 0.0, -jnp.inf)
            bias4 = jnp.concatenate([bias] * A_REP, axis=1)
        if True:
            hs = slice(j * A_HEAD_DIM, (j + 1) * A_HEAD_DIM)
            qst = q_s[j * A_REP:(j + 1) * A_REP, rows, :].reshape(A_REP * qa, A_HEAD_DIM)
            st = _dot_nt(kwin_s[ra:ra + wlen, hs], qst)
            if not has_state:
                st = st + bias4
            m = jnp.maximum(jnp.max(st, axis=0, keepdims=True), sink_rows[j])
            e = jnp.exp2(st - m)
            den = jnp.sum(e, axis=0, keepdims=True) + jnp.exp2(sink_rows[j] - m)
            ot = _dot(vwt_s[hs, ra:ra + wlen], e.astype(BF16)) * (1.0 / den)
            for r2 in range(A_REP // 2):
                two = jnp.concatenate(
                    [ot[:, (2 * r2) * qa:(2 * r2 + 1) * qa], ot[:, (2 * r2 + 1) * qa:(2 * r2 + 2) * qa]],
                    axis=0)
                h0 = j * A_REP + 2 * r2
                at_s[rows, h0 * A_HEAD_DIM:(h0 + 2) * A_HEAD_DIM] = (
                    _transpose_rows_padded(two).astype(BF16))

    def z_block(jb):
        cols = slice(jb * SIDE_BLK, (jb + 1) * SIDE_BLK)
        zs_s[:, cols] = _silu(_dot(xn_s[...], wz_ref[:, cols])).astype(BF16)

    def gate_block(w_ref, dst, jb):
        cols = slice(jb * SIDE_BLK, (jb + 1) * SIDE_BLK)
        dst[:, cols] = _sigmoid(_dot(xn_s[...], w_ref[:, cols])).astype(BF16)

    side_work = [functools.partial(z_block, jb) for jb in range(D_INNER // SIDE_BLK)]
    side_work += [functools.partial(gate_block, wgm_ref, gm_s, jb) for jb in range(D_MODEL // SIDE_BLK)]
    side_work += [functools.partial(gate_block, wga_ref, ga_s, jb) for jb in range(D_MODEL // SIDE_BLK)]

    def out_norm(i):
        rows = slice(i * qa, (i + 1) * qa)
        ym_s[rows, :] = _rms(ym_s[rows, :].astype(F32) * zs_s[rows, :].astype(F32),
                             mng_ref[...]).astype(BF16)

    def out_mix(i):
        rows = slice(i * qa, (i + 1) * qa)
        br_m = _dot(ym_s[rows, :], wbm_ref[:, 0:D_MODEL])
        br_a = _dot(at_s[rows, :], wba_ref[:, 0:D_MODEL])
        gm_s[rows, :] = (gm_s[rows, :].astype(F32) * br_m
                         + ga_s[rows, :].astype(F32) * br_a).astype(BF16)

    def out_proj(i):
        rows = slice(i * qa, (i + 1) * qa)
        y_ref[rows, :] = x_ref[rows, :] + _rms(_dot(gm_s[rows, :], wo_ref[:, 0:D_MODEL]),
                                               gpost_ref[...])

    scan = [functools.partial(ssd_group, c, g) for c in range(nch) for g in range(M_GROUPS)]
    heads = [functools.partial(attn_head, i, j) for i in range(tm // qa) for j in range(A_KV)]
    units = []
    for k, unit in enumerate(scan):
        units.append(unit)
        units += heads[min(len(heads), 2 * len(heads) * k // len(scan)):
                       min(len(heads), 2 * len(heads) * (k + 1) // len(scan))]
    done = 0
    for u, unit in enumerate(units):
        unit()
        upto = -(-len(side_work) * (u + 1) // len(units))
        for side in side_work[done:upto]:
            side()
        done = upto
    for i in range(tm // qa):
        out_norm(i)
    for i in range(tm // qa):
        out_mix(i)
    for i in range(tm // qa):
        out_proj(i)

    if not has_state:
        kwin_s[0:WINDOW, :] = kwin_s[tm:tm + WINDOW, :]
        vwt_s[:, 0:WINDOW] = vwt_s[:, tm:tm + WINDOW]

    @pl.when(si == ns - 1)
    def _():
        for h in range(M_HEADS):
            g, r = divmod(h, M_HPG)
            ssmo_ref[0, h] = _transpose_rows_padded(ht_s[g, :, r * M_HEAD_DIM:(r + 1) * M_HEAD_DIM])


def _mixer(x3d, tables, state, wts, *, tm, q, qa, n_keep):
    bsz, seq, _ = x3d.shape
    assert seq % tm == 0 and tm % qa == 0 and qa % q == 0 and q % BF16_ROWS == 0
    assert tm >= CONV_W - 1
    ns = seq // tm
    nch = tm // q
    has_state = state is not None
    if has_state:
        assert ns == 1 and n_keep == tm and qa == tm
    else:
        assert tm >= WINDOW and n_keep == WINDOW and seq >= WINDOW and q == CHUNK and qa % CHUNK == 0

    in_specs = [pl.BlockSpec((None, tm, D_MODEL), lambda b, s: (b, s, 0))]
    in_specs += [pl.BlockSpec((tm, LANES), lambda b, s: (s, 0))] * 3
    args = [x3d, *tables]
    if has_state:
        conv0, ssm0, ck, cv = state
        assert ck.shape[1] == WINDOW
        in_specs += [
            pl.BlockSpec((1, CONV_W - 1, CONV_DIM), lambda b, s: (b, 0, 0)),
            pl.BlockSpec((1, M_HEADS, M_HEAD_DIM, D_STATE), lambda b, s: (b, 0, 0, 0)),
            pl.BlockSpec((1, WINDOW, KV_WIDTH), lambda b, s: (b, 0, 0)),
            pl.BlockSpec((1, WINDOW, KV_WIDTH), lambda b, s: (b, 0, 0)),
        ]
        args += [conv0, ssm0, ck, cv]
    for name, w in wts:
        if name == "sink":
            in_specs.append(pl.BlockSpec(memory_space=pltpu.SMEM))
        else:
            in_specs.append(_const_spec(w.shape))
        args.append(w)

    out_shape = [
        jax.ShapeDtypeStruct((bsz, seq, D_MODEL), F32),
        jax.ShapeDtypeStruct((bsz, CONV_W - 1, CONV_DIM), F32),
        jax.ShapeDtypeStruct((bsz, M_HEADS, M_HEAD_DIM, D_STATE), F32),
        jax.ShapeDtypeStruct((bsz, n_keep, KV_WIDTH), F32),
        jax.ShapeDtypeStruct((bsz, n_keep, KV_WIDTH), F32),
    ]
    out_specs = [
        pl.BlockSpec((None, tm, D_MODEL), lambda b, s: (b, s, 0)),
        pl.BlockSpec((1, CONV_W - 1, CONV_DIM), lambda b, s: (b, 0, 0)),
        pl.BlockSpec((1, M_HEADS, M_HEAD_DIM, D_STATE), lambda b, s: (b, 0, 0, 0)),
        pl.BlockSpec((1, n_keep, KV_WIDTH), lambda b, s: (b, 0, 0)),
        pl.BlockSpec((1, n_keep, KV_WIDTH), lambda b, s: (b, 0, 0)),
    ]
    w2 = 2 * q
    scratch = [
        pltpu.VMEM((CONV_PAD + tm, CONV_DIM), F32),
        pltpu.VMEM((nch, w2, D_INNER), BF16),
        pltpu.VMEM((tm, BC_W), BF16),
        pltpu.VMEM((nch, BC_W, q), BF16),
        pltpu.VMEM((tm, HP), F32),
        pltpu.VMEM((nch, HP, w2), F32),
        pltpu.VMEM((nch, HP, w2), F32),
        pltpu.VMEM((tm, D_INNER), BF16),
        pltpu.VMEM((tm, D_INNER), BF16),
        pltpu.VMEM((DEC_ROWS, D_INNER), F32),
        pltpu.VMEM((A_HEADS, tm, A_HEAD_DIM), BF16),
        pltpu.VMEM((WINDOW + tm, KV_WIDTH), BF16),
        pltpu.VMEM((KV_WIDTH, WINDOW + tm), BF16),
        pltpu.VMEM((M_GROUPS, D_STATE, GW), F32),
        pltpu.VMEM((tm, D_MODEL), BF16),
        pltpu.VMEM((tm, D_INNER), BF16),
        pltpu.VMEM((tm, A_WIDTH), BF16),
        pltpu.VMEM((tm, D_INNER), BF16),
        pltpu.VMEM((tm, D_MODEL), BF16),
        pltpu.VMEM((tm, D_MODEL), BF16),
    ]
    kern = functools.partial(_mixer_kernel, tm=tm, q=q, qa=qa, n_keep=n_keep, has_state=has_state)
    return pl.pallas_call(
        kern,
        grid=(bsz, ns),
        in_specs=in_specs,
        out_specs=out_specs,
        out_shape=out_shape,
        scratch_shapes=scratch,
        compiler_params=pltpu.CompilerParams(
            dimension_semantics=("arbitrary", "arbitrary"), vmem_limit_bytes=VMEM_LIMIT_BYTES),
        name="mixer_state" if has_state else "mixer_prompt",
    )(*args)


def _rope_tables(pos):
    half = ROT_DIM // 2
    inv_freq = ROPE_THETA ** (-jnp.arange(0, ROT_DIM, 2, dtype=F32) / ROT_DIM)
    ang = pos.astype(F32)[:, None] * inv_freq[None, :]
    cos, sin = jnp.cos(ang), jnp.sin(ang)
    n = pos.shape[0]
    ones = jnp.ones((n, A_HEAD_DIM - ROT_DIM), F32)
    zeros = jnp.zeros((n, A_HEAD_DIM - ROT_DIM), F32)
    zh = jnp.zeros((n, half), F32)
    cos_t = jnp.concatenate([cos, cos, ones], axis=1)
    sa_t = jnp.concatenate([-sin, zh, zeros], axis=1)
    sb_t = jnp.concatenate([zh, sin, zeros], axis=1)
    rep = LANES // A_HEAD_DIM
    return tuple(jnp.tile(t, (1, rep)) for t in (cos_t, sa_t, sb_t))


def _matmul_weight(w):
    w = w.astype(BF16)
    if (w.shape[1] // LANES) % 4 == 0:
        w = jnp.pad(w, ((0, 0), (0, LANES)))
    return w


def _pick_tile(n, pref):
    t = min(n, pref)
    while n % t:
        t //= 2
    return t


def kernel(x_prompt, x_sample, state_conv, state_ssm, cache_k, cache_v, ffn1_pre_g, ffn1_w_gu, ffn1_w_down, ffn1_post_g, mix_pre_g, w_in, conv_w, conv_b, dt_bias, a_log, d_skip, m_norm_g, attn_sink, w_br_m, w_br_a, w_o, mix_post_g, ffn2_pre_g, ffn2_w_gu, ffn2_w_down, ffn2_post_g):
    depth = w_in.shape[0]
    bp, lp, _ = x_prompt.shape
    bs, ls, _ = x_sample.shape
    pos_p = jnp.arange(lp, dtype=jnp.int32)
    pos_s = PAST_LEN + jnp.arange(ls, dtype=jnp.int32)
    tab_p = _rope_tables(pos_p)
    tab_s = _rope_tables(pos_s)
    expand = (jnp.arange(HP)[:, None] == (jnp.arange(D_INNER)[None, :] // M_HEAD_DIM)).astype(BF16)

    yp, ys = x_prompt, x_sample
    new_p, new_s = [], []
    for l in range(depth):
        def row(v):
            return v[l].reshape(1, -1).astype(F32)

        def ffn_w(w_gu, w_down):
            return (_matmul_weight(w_gu[l][:, :D_FF]), _matmul_weight(w_gu[l][:, D_FF:]),
                    _matmul_weight(w_down[l]))

        f1 = ffn_w(ffn1_w_gu, ffn1_w_down)
        f2 = ffn_w(ffn2_w_gu, ffn2_w_down)
        splits = np.cumsum(IN_SIZES)[:-1]
        wz, wxbc, wdt, wq, wk, wv, wgm, wga = jnp.split(w_in[l], splits, axis=1)
        pad_h = HP - M_HEADS
        wdt = jnp.pad(wdt, ((0, 0), (0, pad_h)))
        dtb = jnp.pad(dt_bias[l].astype(F32), (0, pad_h)).reshape(1, HP)
        arow = jnp.pad(-jnp.exp(a_log[l].astype(F32)) * LOG2E, (0, pad_h)).reshape(1, HP)
        def dsk_for(q):
            return jnp.repeat(d_skip[l].astype(F32).reshape(N_PAIRS, 2), q, axis=1)

        def wts_for(q):
            return [
                ("gpre", row(mix_pre_g)),
                ("wxbc", _matmul_weight(wxbc)), ("wdt", wdt.astype(BF16)),
                ("wq", _matmul_weight(wq)), ("wk", wk.astype(BF16)), ("wv", wv.astype(BF16)),
                ("convw", conv_w[l].astype(F32)), ("convb", row(conv_b)),
                ("dtb", dtb), ("arow", arow), ("dsk", dsk_for(q)), ("expand", _matmul_weight(expand)),
                ("sink", attn_sink[l].astype(F32) * LOG2E),
                ("wz", _matmul_weight(wz)), ("wgm", _matmul_weight(wgm)), ("wga", _matmul_weight(wga)),
                ("mng", row(m_norm_g)), ("wbm", _matmul_weight(w_br_m[l])),
                ("wba", _matmul_weight(w_br_a[l])), ("wo", _matmul_weight(w_o[l])),
                ("gpost", row(mix_post_g)),
            ]

        def layer(x3d, tables, state, tm_ffn, tm_mix, q, qa, n_keep):
            b, s, _ = x3d.shape
            x2d = x3d.reshape(b * s, D_MODEL)
            x2d = _ffn(x2d, row(ffn1_pre_g), *f1, row(ffn1_post_g), _pick_tile(b * s, tm_ffn))
            y3d, conv_o, ssm_o, k_o, v_o = _mixer(
                x2d.reshape(b, s, D_MODEL), tables, state, wts_for(q),
                tm=tm_mix, q=q, qa=qa, n_keep=n_keep)
            y2d = _ffn(y3d.reshape(b * s, D_MODEL), row(ffn2_pre_g), *f2, row(ffn2_post_g),
                       _pick_tile(b * s, tm_ffn))
            return (y2d.reshape(b, s, D_MODEL), conv_o, ssm_o,
                    k_o.reshape(b, n_keep, A_KV, A_HEAD_DIM), v_o.reshape(b, n_keep, A_KV, A_HEAD_DIM))

        rows_p = min(WINDOW, lp)
        yp, *st_p = layer(yp, tab_p, None, 1024, _pick_tile(lp, 256), CHUNK, 2 * CHUNK, rows_p)
        n_cache = cache_k.shape[2]
        state = (state_conv[l].astype(F32), state_ssm[l].astype(F32),
                 cache_k[l].reshape(bs, n_cache, KV_WIDTH), cache_v[l].reshape(bs, n_cache, KV_WIDTH))
        q_s = min(ls, CHUNK)
        ys, *st_s = layer(ys, tab_s, state, 256, ls, q_s, ls, ls)
        new_p.append(st_p)
        new_s.append(st_s)

    conv_p, ssm_p, k_p, v_p = [jnp.stack(t) for t in zip(*new_p)]
    conv_s, ssm_s, k_s, v_s = [jnp.stack(t) for t in zip(*new_s)]
    return (yp, ys, conv_p, ssm_p, k_p, v_p, conv_s, ssm_s, k_s, v_s)
```

```python
import functools

import jax
import jax.numpy as jnp
import numpy as np
from jax import lax
from jax.experimental import pallas as pl
from jax.experimental.pallas import tpu as pltpu

D_MODEL = 1024
CHUNK = 64
EPS = 1e-6
D_INNER = 2 * D_MODEL
M_HEAD_DIM = 64
M_HEADS = D_INNER // M_HEAD_DIM
M_GROUPS = 4
M_HPG = M_HEADS // M_GROUPS
D_STATE = 128
CONV_W = 4
CONV_DIM = D_INNER + 2 * M_GROUPS * D_STATE
A_HEADS = 16
A_KV = 4
A_REP = A_HEADS // A_KV
A_HEAD_DIM = 64
A_WIDTH = A_HEADS * A_HEAD_DIM
KV_WIDTH = A_KV * A_HEAD_DIM
WINDOW = 128
ROT_DIM = A_HEAD_DIM // 4
ROPE_THETA = 500000.0
LOG2E = 1.4426950408889634
D_FF = 2816
PAST_LEN = 1024
IN_SIZES = (D_INNER, CONV_DIM, M_HEADS, A_WIDTH, KV_WIDTH, KV_WIDTH, D_MODEL, D_MODEL)

LANES = 128
SUBLANES = 8
BF16_ROWS = 16
VMEM_LIMIT_BYTES = 60 * 1024 * 1024

HP = LANES
GW = M_HPG * M_HEAD_DIM
BC_W = M_GROUPS * D_STATE
PAIR_W = 2 * M_HEAD_DIM
N_PAIRS = M_HEADS // 2
PAIRS_PER_GROUP = M_HPG // 2
CONV_PAD = SUBLANES
COL_BLK = 512
SIDE_BLK = 256
FFN_PART_ROWS = 256
DEC_ROWS = BF16_ROWS

BF16 = jnp.bfloat16
F32 = jnp.float32


def _dot(a, b):
    return jnp.dot(a, b, preferred_element_type=F32)


def _dot_nt(a, b):
    return lax.dot_general(a, b, (((1,), (1,)), ((), ())), preferred_element_type=F32)


def _rms(x, g):
    return x * lax.rsqrt(jnp.mean(x * x, axis=-1, keepdims=True) + EPS) * g


def _silu(x):
    h = 0.5 * x
    return h + h * jnp.tanh(h)


def _sigmoid(x):
    return 1.0 / (1.0 + jnp.exp(-x))


def _softplus(x):
    return jnp.maximum(x, 0.0) + jnp.log(1.0 + jnp.exp(-jnp.abs(x)))


def _split3(x):
    hi = x.astype(BF16)
    r = x - hi.astype(F32)
    mid = r.astype(BF16)
    lo = (r - mid.astype(F32)).astype(BF16)
    return hi, mid, lo


def _const_spec(shape):
    zeros = (0,) * len(shape)
    return pl.BlockSpec(shape, lambda *_: zeros, pipeline_mode=pl.Buffered(1))


def _ffn_kernel(x_ref, gpre_ref, wg_ref, wu_ref, wd_ref, gpost_ref, o_ref, *, parts):
    rows = x_ref.shape[0] // parts
    sl = [slice(p * rows, (p + 1) * rows) for p in range(parts)]
    xn = [_rms(x_ref[s, :], gpre_ref[...]).astype(BF16) for s in sl]
    act = []
    for p in range(parts):
        gate = _dot(xn[p], wg_ref[:, 0:D_FF])
        up = _dot(xn[p], wu_ref[:, 0:D_FF])
        act.append((_silu(gate) * up).astype(BF16))
    for p in range(parts):
        y = _dot(act[p], wd_ref[:, 0:D_MODEL])
        o_ref[sl[p], :] = x_ref[sl[p], :] + 0.5 * _rms(y, gpost_ref[...])


def _ffn(x2d, gpre, wg, wu, wd, gpost, tm):
    m = x2d.shape[0]
    assert m % tm == 0
    parts = max(1, tm // FFN_PART_ROWS)
    return pl.pallas_call(
        functools.partial(_ffn_kernel, parts=parts),
        grid=(m // tm,),
        in_specs=[
            pl.BlockSpec((tm, D_MODEL), lambda i: (i, 0)),
            _const_spec(gpre.shape), _const_spec(wg.shape), _const_spec(wu.shape),
            _const_spec(wd.shape), _const_spec(gpost.shape),
        ],
        out_specs=pl.BlockSpec((tm, D_MODEL), lambda i: (i, 0)),
        out_shape=jax.ShapeDtypeStruct((m, D_MODEL), F32),
        compiler_params=pltpu.CompilerParams(
            dimension_semantics=("arbitrary",), vmem_limit_bytes=VMEM_LIMIT_BYTES),
        name="ffn_half_step",
    )(x2d, gpre, wg, wu, wd, gpost)


def _rope(x, cos_t, sa_t, sb_t):
    w = x.shape[1]
    reps = w // LANES
    cos_f = jnp.tile(cos_t, (1, reps))
    sa_f = jnp.tile(sa_t, (1, reps))
    sb_f = jnp.tile(sb_t, (1, reps))
    half = ROT_DIM // 2
    return x * cos_f + pltpu.roll(x, w - half, 1) * sa_f + pltpu.roll(x, half, 1) * sb_f


def _transpose_rows_padded(x):
    r, c = x.shape
    rp = -(-r // LANES) * LANES
    cp = -(-c // LANES) * LANES
    if cp != c:
        x = jnp.concatenate([x, jnp.zeros((r, cp - c), x.dtype)], axis=1)
    if rp != r:
        x = jnp.concatenate([x, jnp.zeros((rp - r, cp), x.dtype)], axis=0)
    xt = x.T
    return xt if (rp == r and cp == c) else xt[:c, :r]


def _mixer_kernel(*refs, tm, q, qa, n_keep, has_state):
    it = iter(refs)
    x_ref = next(it)
    cos_ref, sa_ref, sb_ref = next(it), next(it), next(it)
    if has_state:
        conv0_ref, ssm0_ref, ck_ref, cv_ref = next(it), next(it), next(it), next(it)
    (gpre_ref, wxbc_ref, wdt_ref, wq_ref, wk_ref, wv_ref,
     convw_ref, convb_ref, dtb_ref, arow_ref, dsk_ref, expand_ref, sink_ref,
     wz_ref, wgm_ref, wga_ref, mng_ref, wbm_ref, wba_ref, wo_ref, gpost_ref) = [
         next(it) for _ in range(21)]
    y_ref, convo_ref, ssmo_ref, ko_ref, vo_ref = [next(it) for _ in range(5)]
    (cbuf_s, xbd_s, c_s, bt_s, acs_s, acst2_s, dtt2_s, eax_s, xw_s, decx_s,
     q_s, kwin_s, vwt_s, ht_s, xn_s, ym_s, at_s, zs_s, gm_s, ga_s) = [next(it) for _ in range(20)]

    si = pl.program_id(1)
    ns = pl.num_programs(1)
    nch = tm // q
    cpb = qa // q
    w2 = 2 * q
    wlen = WINDOW + qa
    assert nch <= DEC_ROWS

    @pl.when(si == 0)
    def _():
        if has_state:
            cbuf_s[CONV_PAD - (CONV_W - 1):CONV_PAD, :] = conv0_ref[0]
            for h in range(M_HEADS):
                g, r = divmod(h, M_HPG)
                ht_s[g, :, r * M_HEAD_DIM:(r + 1) * M_HEAD_DIM] = _transpose_rows_padded(ssm0_ref[0, h])
            kwin_s[0:WINDOW, :] = ck_ref[0].astype(BF16)
            vwt_s[:, 0:WINDOW] = _transpose_rows_padded(cv_ref[0]).astype(BF16)
        else:
            cbuf_s[0:CONV_PAD, :] = jnp.zeros((CONV_PAD, CONV_DIM), F32)
            ht_s[...] = jnp.zeros(ht_s.shape, F32)
            kwin_s[0:WINDOW, :] = jnp.zeros((WINDOW, KV_WIDTH), BF16)
            vwt_s[:, 0:WINDOW] = jnp.zeros((KV_WIDTH, WINDOW), BF16)

    x = x_ref[...]
    xn = _rms(x, gpre_ref[...]).astype(BF16)
    xn_s[...] = xn

    cos_t, sa_t, sb_t = cos_ref[...], sa_ref[...], sb_ref[...]
    lane_b = lax.broadcasted_iota(jnp.int32, (tm, COL_BLK), 1)
    even_head = (lane_b % PAIR_W) < M_HEAD_DIM
    d0 = 2 * tm
    n_xb = D_INNER // COL_BLK
    n_bb = BC_W // COL_BLK
    st = {}

    def dt_cumsum():
        dt = _softplus(_dot(xn, wdt_ref[...]) + dtb_ref[...])
        rr = lax.broadcasted_iota(jnp.int32, (tm, tm), 0)
        cc = lax.broadcasted_iota(jnp.int32, (tm, tm), 1)
        tri = jnp.where((rr >= cc) & ((rr // q) == (cc // q)), 1.0, 0.0).astype(BF16)
        cs3 = _dot(tri, jnp.concatenate(_split3(dt * arow_ref[...]), axis=1))
        acs = cs3[:, 0:HP] + cs3[:, HP:2 * HP] + cs3[:, 2 * HP:3 * HP]
        acs_s[...] = acs
        st["dt"], st["acs"] = dt, acs

    def dt_factors():
        dt, acs = st["dt"], st["acs"]
        lasts = [acs[c * q + q - 1:c * q + q, :] for c in range(nch)]
        last_b = jnp.concatenate([jnp.broadcast_to(l, (q, HP)) for l in lasts], axis=0)
        eacs = jnp.exp2(acs)
        wend = dt * jnp.exp2(last_b - acs)
        dec = jnp.exp2(jnp.concatenate(lasts + [jnp.zeros((DEC_ROWS - nch, HP), F32)], axis=0))
        st["lhs"] = jnp.concatenate([eacs.astype(BF16), wend.astype(BF16), *_split3(dec)], axis=0)

    def pair_rows(name, dst):
        a = st[name]
        nxt = pltpu.roll(a, HP - 1, 1)
        z = jnp.concatenate(
            [p[c * q:(c + 1) * q] for c in range(nch) for p in (a, nxt)], axis=0)
        zt = _transpose_rows_padded(z)
        for c in range(nch):
            dst[c] = zt[:, c * w2:(c + 1) * w2]

    def proj(jb):
        cols = slice(jb * COL_BLK, (jb + 1) * COL_BLK)
        cbuf_s[CONV_PAD:CONV_PAD + tm, cols] = _dot(xn, wxbc_ref[:, cols])

    def conv(jb):
        cols = slice(jb * COL_BLK, (jb + 1) * COL_BLK)
        acc = convb_ref[:, cols] + (cbuf_s[CONV_PAD:CONV_PAD + tm, cols]
                                    * convw_ref[CONV_W - 1:CONV_W, cols])
        for j in range(CONV_W - 1):
            off = CONV_PAD - (CONV_W - 1) + j
            acc = acc + cbuf_s[off:off + tm, cols] * convw_ref[j:j + 1, cols]
        return _silu(acc)

    def x_block(jb):
        cols = slice(jb * COL_BLK, (jb + 1) * COL_BLK)
        xc = conv(jb)
        xe = jnp.where(even_head, xc, 0.0).astype(BF16)
        xo = jnp.where(even_head, 0.0, xc).astype(BF16)
        for c in range(nch):
            xbd_s[c, 0:q, cols] = xe[c * q:(c + 1) * q]
            xbd_s[c, q:w2, cols] = xo[c * q:(c + 1) * q]
        ex = _dot(st["lhs"], expand_ref[:, cols])
        eax_s[:, cols] = ex[0:tm].astype(BF16)
        xw_s[:, cols] = (xc * ex[tm:d0]).astype(BF16)
        decx_s[:, cols] = (ex[d0:d0 + DEC_ROWS] + ex[d0 + DEC_ROWS:d0 + 2 * DEC_ROWS]
                           + ex[d0 + 2 * DEC_ROWS:d0 + 3 * DEC_ROWS])

    def b_block(jb):
        btt = _transpose_rows_padded(conv(jb))
        r = slice((jb - n_xb) * COL_BLK, (jb - n_xb + 1) * COL_BLK)
        for c in range(nch):
            bt_s[c, r, :] = btt[:, c * q:(c + 1) * q].astype(BF16)

    def c_block(jb):
        r = slice((jb - n_xb - n_bb) * COL_BLK, (jb - n_xb - n_bb + 1) * COL_BLK)
        c_s[:, r] = conv(jb).astype(BF16)

    def q_block(jq):
        cols = slice(jq * COL_BLK, (jq + 1) * COL_BLK)
        qv = _rope(_dot(xn, wq_ref[:, cols]), cos_t, sa_t, sb_t) * (LOG2E * A_HEAD_DIM ** -0.5)
        per = COL_BLK // A_HEAD_DIM
        for hh in range(per):
            q_s[jq * per + hh] = qv[:, hh * A_HEAD_DIM:(hh + 1) * A_HEAD_DIM].astype(BF16)

    def kv_block():
        kv = _rope(_dot(xn, wk_ref[...]), cos_t, sa_t, sb_t)
        vv = _dot(xn, wv_ref[...])
        kwin_s[WINDOW:WINDOW + tm, :] = kv.astype(BF16)
        vwt_s[:, WINDOW:WINDOW + tm] = _transpose_rows_padded(vv).astype(BF16)
        ko_ref[0] = kv[tm - n_keep:, :]
        vo_ref[0] = vv[tm - n_keep:, :]

    P = functools.partial
    bc = list(range(n_xb, n_xb + 2 * n_bb))
    fin = [P(b_block, j) if j < n_xb + n_bb else P(c_block, j) for j in bc]
    light = [dt_cumsum, dt_factors, P(pair_rows, "acs", acst2_s), P(pair_rows, "dt", dtt2_s)]
    light += [P(q_block, jq) for jq in range(A_WIDTH // COL_BLK)] + [kv_block]
    order = [P(proj, bc[0])]
    blocks = bc + list(range(n_xb))
    fins = fin + [P(x_block, j) for j in range(n_xb)]
    for i in range(len(blocks)):
        if i + 1 < len(blocks):
            order.append(P(proj, blocks[i + 1]))
        order.append(fins[i])
        if light:
            order.append(light.pop(0))
    order += light
    for stage in order:
        stage()
    tail = cbuf_s[CONV_PAD + tm - (CONV_W - 1):CONV_PAD + tm, :]
    convo_ref[0] = tail
    cbuf_s[CONV_PAD - (CONV_W - 1):CONV_PAD, :] = tail

    row2 = lax.broadcasted_iota(jnp.int32, (q, w2), 0)
    lane2 = lax.broadcasted_iota(jnp.int32, (q, w2), 1)
    first_half = lane2 < q
    pos2 = jnp.where(first_half, lane2, lane2 - q)
    row_g = lax.broadcasted_iota(jnp.int32, (q, PAIRS_PER_GROUP * w2), 0)
    pos_g = lax.broadcasted_iota(jnp.int32, (q, PAIRS_PER_GROUP * w2), 1) % q
    causal_g = row_g >= pos_g
    diag_g = row_g == pos_g
    if not has_state:
        krow = lax.broadcasted_iota(jnp.int32, (wlen, qa), 0)
        qcol = lax.broadcasted_iota(jnp.int32, (wlen, qa), 1)
        dchunk = krow // CHUNK - qcol // CHUNK
        band = (dchunk >= 0) & (dchunk <= WINDOW // CHUNK)
    sink_rows = [
        jnp.concatenate([jnp.full((1, qa), sink_ref[j * A_REP + r], F32) for r in range(A_REP)],
                        axis=1) for j in range(A_KV)]

    def ssd_group(c, g):
        rows = slice(c * q, (c + 1) * q)
        acs_c = acs_s[rows, :]
        if True:
            bt_g = bt_s[c, g * D_STATE:(g + 1) * D_STATE, :]
            c_g = c_s[rows, g * D_STATE:(g + 1) * D_STATE]
            cb = _dot(c_g, bt_g)
            cb2 = jnp.concatenate([cb, cb], axis=1)
            ystate = _dot(c_g, ht_s[g].astype(BF16))
            gl = slice(g * GW, (g + 1) * GW)
            pairs = range(g * PAIRS_PER_GROUP, (g + 1) * PAIRS_PER_GROUP)
            col = jnp.concatenate(
                [jnp.where(first_half, acs_c[:, 2 * k:2 * k + 1], acs_c[:, 2 * k + 1:2 * k + 2])
                 for k in pairs], axis=1)
            arow = jnp.concatenate([acst2_s[c, 2 * k:2 * k + 1, :] for k in pairs], axis=1)
            drow = jnp.concatenate([dtt2_s[c, 2 * k:2 * k + 1, :] for k in pairs], axis=1)
            dskw = jnp.concatenate([dsk_ref[k:k + 1, 0:w2] for k in pairs], axis=1)
            seg = jnp.where(causal_g, jnp.exp2(col - arow), 0.0)
            m2 = seg * (jnp.concatenate([cb2] * PAIRS_PER_GROUP, axis=1) * drow)
            m2 = (m2 + jnp.where(diag_g, dskw, 0.0)).astype(BF16)
            y2 = jnp.concatenate(
                [_dot(m2[:, kk * w2:(kk + 1) * w2], xbd_s[c, :, k * PAIR_W:(k + 1) * PAIR_W])
                 for kk, k in enumerate(pairs)], axis=1)
            ym_s[rows, gl] = (y2 + ystate * eax_s[rows, gl].astype(F32)).astype(BF16)
            ht_s[g] = ht_s[g] * decx_s[c:c + 1, gl] + _dot(bt_g, xw_s[rows, gl])

    def attn_head(i, j):
        ra = i * qa
        rows = slice(ra, ra + qa)
        if not has_state:
            ok = band if ra >= WINDOW else band & ((krow >= WINDOW - ra) | (si > 0))
            bias = jnp.where(ok, 0.0, -jnp.inf)
            bias4 = jnp.concatenate([bias] * A_REP, axis=1)
        hs = slice(j * A_HEAD_DIM, (j + 1) * A_HEAD_DIM)
        qst = q_s[j * A_REP:(j + 1) * A_REP, rows, :].reshape(A_REP * qa, A_HEAD_DIM)
        st = _dot_nt(kwin_s[ra:ra + wlen, hs], qst)
        if not has_state:
            st = st + bias4
        m = jnp.maximum(jnp.max(st, axis=0, keepdims=True), sink_rows[j])
        e = jnp.exp2(st - m)
        den = jnp.sum(e, axis=0, keepdims=True) + jnp.exp2(sink_rows[j] - m)
        ot = _dot(vwt_s[hs, ra:ra + wlen], e.astype(BF16)) * (1.0 / den)
        for r2 in range(A_REP // 2):
            two = jnp.concatenate(
                [ot[:, (2 * r2) * qa:(2 * r2 + 1) * qa], ot[:, (2 * r2 + 1) * qa:(2 * r2 + 2) * qa]],
                axis=0)
            h0 = j * A_REP + 2 * r2
            at_s[rows, h0 * A_HEAD_DIM:(h0 + 2) * A_HEAD_DIM] = (
                _transpose_rows_padded(two).astype(BF16))

    def z_block(jb):
        cols = slice(jb * SIDE_BLK, (jb + 1) * SIDE_BLK)
        zs_s[:, cols] = _silu(_dot(xn_s[...], wz_ref[:, cols])).astype(BF16)

    def gate_block(w_ref, dst, jb):
        cols = slice(jb * SIDE_BLK, (jb + 1) * SIDE_BLK)
        dst[:, cols] = _sigmoid(_dot(xn_s[...], w_ref[:, cols])).astype(BF16)

    side_work = [functools.partial(z_block, jb) for jb in range(D_INNER // SIDE_BLK)]
    side_work += [functools.partial(gate_block, wgm_ref, gm_s, jb) for jb in range(D_MODEL // SIDE_BLK)]
    side_work += [functools.partial(gate_block, wga_ref, ga_s, jb) for jb in range(D_MODEL // SIDE_BLK)]

    def out_norm(i):
        rows = slice(i * qa, (i + 1) * qa)
        ym_s[rows, :] = _rms(ym_s[rows, :].astype(F32) * zs_s[rows, :].astype(F32),
                             mng_ref[...]).astype(BF16)

    def out_mix(i):
        rows = slice(i * qa, (i + 1) * qa)
        br_m = _dot(ym_s[rows, :], wbm_ref[:, 0:D_MODEL])
        br_a = _dot(at_s[rows, :], wba_ref[:, 0:D_MODEL])
        gm_s[rows, :] = (gm_s[rows, :].astype(F32) * br_m
                         + ga_s[rows, :].astype(F32) * br_a).astype(BF16)

    def out_proj(i):
        rows = slice(i * qa, (i + 1) * qa)
        y_ref[rows, :] = x_ref[rows, :] + _rms(_dot(gm_s[rows, :], wo_ref[:, 0:D_MODEL]),
                                               gpost_ref[...])

    scan = [functools.partial(ssd_group, c, g) for c in range(nch) for g in range(M_GROUPS)]
    heads = [functools.partial(attn_head, i, j) for i in range(tm // qa) for j in range(A_KV)]
    units = []
    for k, unit in enumerate(scan):
        units.append(unit)
        units += heads[min(len(heads), 2 * len(heads) * k // len(scan)):
                       min(len(heads), 2 * len(heads) * (k + 1) // len(scan))]
    done = 0
    for u, unit in enumerate(units):
        unit()
        upto = -(-len(side_work) * (u + 1) // len(units))
        for side in side_work[done:upto]:
            side()
        done = upto
    for i in range(tm // qa):
        out_norm(i)
    for i in range(tm // qa):
        out_mix(i)
    for i in range(tm // qa):
        out_proj(i)

    if not has_state:
        kwin_s[0:WINDOW, :] = kwin_s[tm:tm + WINDOW, :]
        vwt_s[:, 0:WINDOW] = vwt_s[:, tm:tm + WINDOW]

    @pl.when(si == ns - 1)
    def _():
        for h in range(M_HEADS):
            g, r = divmod(h, M_HPG)
            ssmo_ref[0, h] = _transpose_rows_padded(ht_s[g, :, r * M_HEAD_DIM:(r + 1) * M_HEAD_DIM])


def _mixer(x3d, tables, state, wts, *, tm, q, qa, n_keep):
    bsz, seq, _ = x3d.shape
    assert seq % tm == 0 and tm % qa == 0 and qa % q == 0 and q % BF16_ROWS == 0
    assert tm >= CONV_PAD >= CONV_W - 1
    ns = seq // tm
    nch = tm // q
    has_state = state is not None
    if has_state:
        assert ns == 1 and n_keep == tm and qa == tm
    else:
        assert tm >= WINDOW and n_keep == WINDOW and seq >= WINDOW and q == CHUNK and qa % CHUNK == 0

    in_specs = [pl.BlockSpec((None, tm, D_MODEL), lambda b, s: (b, s, 0))]
    in_specs += [pl.BlockSpec((tm, LANES), lambda b, s: (s, 0))] * 3
    args = [x3d, *tables]
    if has_state:
        conv0, ssm0, ck, cv = state
        assert ck.shape[1] == WINDOW
        in_specs += [
            pl.BlockSpec((1, CONV_W - 1, CONV_DIM), lambda b, s: (b, 0, 0)),
            pl.BlockSpec((1, M_HEADS, M_HEAD_DIM, D_STATE), lambda b, s: (b, 0, 0, 0)),
            pl.BlockSpec((1, WINDOW, KV_WIDTH), lambda b, s: (b, 0, 0)),
            pl.BlockSpec((1, WINDOW, KV_WIDTH), lambda b, s: (b, 0, 0)),
        ]
        args += [conv0, ssm0, ck, cv]
    for name, w in wts:
        if name == "sink":
            in_specs.append(pl.BlockSpec(memory_space=pltpu.SMEM))
        else:
            in_specs.append(_const_spec(w.shape))
        args.append(w)

    out_shape = [
        jax.ShapeDtypeStruct((bsz, seq, D_MODEL), F32),
        jax.ShapeDtypeStruct((bsz, CONV_W - 1, CONV_DIM), F32),
        jax.ShapeDtypeStruct((bsz, M_HEADS, M_HEAD_DIM, D_STATE), F32),
        jax.ShapeDtypeStruct((bsz, n_keep, KV_WIDTH), F32),
        jax.ShapeDtypeStruct((bsz, n_keep, KV_WIDTH), F32),
    ]
    out_specs = [
        pl.BlockSpec((None, tm, D_MODEL), lambda b, s: (b, s, 0)),
        pl.BlockSpec((1, CONV_W - 1, CONV_DIM), lambda b, s: (b, 0, 0)),
        pl.BlockSpec((1, M_HEADS, M_HEAD_DIM, D_STATE), lambda b, s: (b, 0, 0, 0)),
        pl.BlockSpec((1, n_keep, KV_WIDTH), lambda b, s: (b, 0, 0)),
        pl.BlockSpec((1, n_keep, KV_WIDTH), lambda b, s: (b, 0, 0)),
    ]
    w2 = 2 * q
    scratch = [
        pltpu.VMEM((CONV_PAD + tm, CONV_DIM), F32),
        pltpu.VMEM((nch, w2, D_INNER), BF16),
        pltpu.VMEM((tm, BC_W), BF16),
        pltpu.VMEM((nch, BC_W, q), BF16),
        pltpu.VMEM((tm, HP), F32),
        pltpu.VMEM((nch, HP, w2), F32),
        pltpu.VMEM((nch, HP, w2), F32),
        pltpu.VMEM((tm, D_INNER), BF16),
        pltpu.VMEM((tm, D_INNER), BF16),
        pltpu.VMEM((DEC_ROWS, D_INNER), F32),
        pltpu.VMEM((A_HEADS, tm, A_HEAD_DIM), BF16),
        pltpu.VMEM((WINDOW + tm, KV_WIDTH), BF16),
        pltpu.VMEM((KV_WIDTH, WINDOW + tm), BF16),
        pltpu.VMEM((M_GROUPS, D_STATE, GW), F32),
        pltpu.VMEM((tm, D_MODEL), BF16),
        pltpu.VMEM((tm, D_INNER), BF16),
        pltpu.VMEM((tm, A_WIDTH), BF16),
        pltpu.VMEM((tm, D_INNER), BF16),
        pltpu.VMEM((tm, D_MODEL), BF16),
        pltpu.VMEM((tm, D_MODEL), BF16),
    ]
    kern = functools.partial(_mixer_kernel, tm=tm, q=q, qa=qa, n_keep=n_keep, has_state=has_state)
    return pl.pallas_call(
        kern,
        grid=(bsz, ns),
        in_specs=in_specs,
        out_specs=out_specs,
        out_shape=out_shape,
        scratch_shapes=scratch,
        compiler_params=pltpu.CompilerParams(
            dimension_semantics=("arbitrary", "arbitrary"), vmem_limit_bytes=VMEM_LIMIT_BYTES),
        name="mixer_state" if has_state else "mixer_prompt",
    )(*args)


def _rope_tables(pos):
    half = ROT_DIM // 2
    inv_freq = ROPE_THETA ** (-jnp.arange(0, ROT_DIM, 2, dtype=F32) / ROT_DIM)
    ang = pos.astype(F32)[:, None] * inv_freq[None, :]
    cos, sin = jnp.cos(ang), jnp.sin(ang)
    n = pos.shape[0]
    ones = jnp.ones((n, A_HEAD_DIM - ROT_DIM), F32)
    zeros = jnp.zeros((n, A_HEAD_DIM - ROT_DIM), F32)
    zh = jnp.zeros((n, half), F32)
    cos_t = jnp.concatenate([cos, cos, ones], axis=1)
    sa_t = jnp.concatenate([-sin, zh, zeros], axis=1)
    sb_t = jnp.concatenate([zh, sin, zeros], axis=1)
    rep = LANES // A_HEAD_DIM
    return tuple(jnp.tile(t, (1, rep)) for t in (cos_t, sa_t, sb_t))


def _matmul_weight(w):
    w = w.astype(BF16)
    if (w.shape[1] // LANES) % 4 == 0:
        w = jnp.pad(w, ((0, 0), (0, LANES)))
    return w


def _pick_tile(n, pref):
    t = min(n, pref)
    while n % t:
        t //= 2
    return t


def kernel(x_prompt, x_sample, state_conv, state_ssm, cache_k, cache_v, ffn1_pre_g, ffn1_w_gu, ffn1_w_down, ffn1_post_g, mix_pre_g, w_in, conv_w, conv_b, dt_bias, a_log, d_skip, m_norm_g, attn_sink, w_br_m, w_br_a, w_o, mix_post_g, ffn2_pre_g, ffn2_w_gu, ffn2_w_down, ffn2_post_g):
    depth = w_in.shape[0]
    bp, lp, _ = x_prompt.shape
    bs, ls, _ = x_sample.shape
    pos_p = jnp.arange(lp, dtype=jnp.int32)
    pos_s = PAST_LEN + jnp.arange(ls, dtype=jnp.int32)
    tab_p = _rope_tables(pos_p)
    tab_s = _rope_tables(pos_s)
    expand = (jnp.arange(HP)[:, None] == (jnp.arange(D_INNER)[None, :] // M_HEAD_DIM)).astype(BF16)

    yp, ys = x_prompt, x_sample
    new_p, new_s = [], []
    for l in range(depth):
        def row(v):
            return v[l].reshape(1, -1).astype(F32)

        def ffn_w(w_gu, w_down):
            return (_matmul_weight(w_gu[l][:, :D_FF]), _matmul_weight(w_gu[l][:, D_FF:]),
                    _matmul_weight(w_down[l]))

        f1 = ffn_w(ffn1_w_gu, ffn1_w_down)
        f2 = ffn_w(ffn2_w_gu, ffn2_w_down)
        splits = np.cumsum(IN_SIZES)[:-1]
        wz, wxbc, wdt, wq, wk, wv, wgm, wga = jnp.split(w_in[l], splits, axis=1)
        pad_h = HP - M_HEADS
        wdt = jnp.pad(wdt, ((0, 0), (0, pad_h)))
        dtb = jnp.pad(dt_bias[l].astype(F32), (0, pad_h)).reshape(1, HP)
        arow = jnp.pad(-jnp.exp(a_log[l].astype(F32)) * LOG2E, (0, pad_h)).reshape(1, HP)
        def dsk_for(q):
            return jnp.repeat(d_skip[l].astype(F32).reshape(N_PAIRS, 2), q, axis=1)

        def wts_for(q):
            return [
                ("gpre", row(mix_pre_g)),
                ("wxbc", _matmul_weight(wxbc)), ("wdt", wdt.astype(BF16)),
                ("wq", _matmul_weight(wq)), ("wk", wk.astype(BF16)), ("wv", wv.astype(BF16)),
                ("convw", conv_w[l].astype(F32)), ("convb", row(conv_b)),
                ("dtb", dtb), ("arow", arow), ("dsk", dsk_for(q)), ("expand", _matmul_weight(expand)),
                ("sink", attn_sink[l].astype(F32) * LOG2E),
                ("wz", _matmul_weight(wz)), ("wgm", _matmul_weight(wgm)), ("wga", _matmul_weight(wga)),
                ("mng", row(m_norm_g)), ("wbm", _matmul_weight(w_br_m[l])),
                ("wba", _matmul_weight(w_br_a[l])), ("wo", _matmul_weight(w_o[l])),
                ("gpost", row(mix_post_g)),
            ]

        def layer(x3d, tables, state, tm_ffn, tm_mix, q, qa, n_keep):
            b, s, _ = x3d.shape
            x2d = x3d.reshape(b * s, D_MODEL)
            x2d = _ffn(x2d, row(ffn1_pre_g), *f1, row(ffn1_post_g), _pick_tile(b * s, tm_ffn))
            y3d, conv_o, ssm_o, k_o, v_o = _mixer(
                x2d.reshape(b, s, D_MODEL), tables, state, wts_for(q),
                tm=tm_mix, q=q, qa=qa, n_keep=n_keep)
            y2d = _ffn(y3d.reshape(b * s, D_MODEL), row(ffn2_pre_g), *f2, row(ffn2_post_g),
                       _pick_tile(b * s, tm_ffn))
            return (y2d.reshape(b, s, D_MODEL), conv_o, ssm_o,
                    k_o.reshape(b, n_keep, A_KV, A_HEAD_DIM), v_o.reshape(b, n_keep, A_KV, A_HEAD_DIM))

        rows_p = min(WINDOW, lp)
        yp, *st_p = layer(yp, tab_p, None, 1024, _pick_tile(lp, 256), CHUNK, 2 * CHUNK, rows_p)
        n_cache = cache_k.shape[2]
        def of_layer(a):
            return a.reshape(a.shape[1:]) if depth == 1 else a[l]

        state = (of_layer(state_conv).astype(F32), of_layer(state_ssm).astype(F32),
                 of_layer(cache_k).reshape(bs, n_cache, KV_WIDTH),
                 of_layer(cache_v).reshape(bs, n_cache, KV_WIDTH))
        q_s = min(ls, CHUNK)
        ys, *st_s = layer(ys, tab_s, state, 256, ls, q_s, ls, ls)
        new_p.append(st_p)
        new_s.append(st_s)

    conv_p, ssm_p, k_p, v_p = [jnp.stack(t) for t in zip(*new_p)]
    conv_s, ssm_s, k_s, v_s = [jnp.stack(t) for t in zip(*new_s)]
    return (yp, ys, conv_p, ssm_p, k_p, v_p, conv_s, ssm_s, k_s, v_s)
```

```python
import functools

import jax
import jax.numpy as jnp
import numpy as np
from jax import lax
from jax.experimental import pallas as pl
from jax.experimental.pallas import tpu as pltpu

D_MODEL = 1024
CHUNK = 64
EPS = 1e-6
D_INNER = 2 * D_MODEL
M_HEAD_DIM = 64
M_HEADS = D_INNER // M_HEAD_DIM
M_GROUPS = 4
M_HPG = M_HEADS // M_GROUPS
D_STATE = 128
CONV_W = 4
CONV_DIM = D_INNER + 2 * M_GROUPS * D_STATE
A_HEADS = 16
A_KV = 4
A_REP = A_HEADS // A_KV
A_HEAD_DIM = 64
A_WIDTH = A_HEADS * A_HEAD_DIM
KV_WIDTH = A_KV * A_HEAD_DIM
WINDOW = 128
ROT_DIM = A_HEAD_DIM // 4
ROPE_THETA = 500000.0
LOG2E = 1.4426950408889634
D_FF = 2816
PAST_LEN = 1024
IN_SIZES = (D_INNER, CONV_DIM, M_HEADS, A_WIDTH, KV_WIDTH, KV_WIDTH, D_MODEL, D_MODEL)

LANES = 128
SUBLANES = 8
BF16_ROWS = 16
VMEM_LIMIT_BYTES = 60 * 1024 * 1024

HP = LANES
GW = M_HPG * M_HEAD_DIM
BC_W = M_GROUPS * D_STATE
PAIR_W = 2 * M_HEAD_DIM
N_PAIRS = M_HEADS // 2
PAIRS_PER_GROUP = M_HPG // 2
CONV_PAD = SUBLANES
COL_BLK = 256
SIDE_BLK = 256
FFN_PART_ROWS = 256
DEC_ROWS = BF16_ROWS

BF16 = jnp.bfloat16
F32 = jnp.float32


def _dot(a, b):
    return jnp.dot(a, b, preferred_element_type=F32)


def _dot_nt(a, b):
    return lax.dot_general(a, b, (((1,), (1,)), ((), ())), preferred_element_type=F32)


def _rms(x, g):
    return x * lax.rsqrt(jnp.mean(x * x, axis=-1, keepdims=True) + EPS) * g


def _silu(x):
    h = 0.5 * x
    return h + h * jnp.tanh(h)


def _sigmoid(x):
    return 1.0 / (1.0 + jnp.exp(-x))


def _softplus(x):
    return jnp.maximum(x, 0.0) + jnp.log(1.0 + jnp.exp(-jnp.abs(x)))


def _split3(x):
    hi = x.astype(BF16)
    r = x - hi.astype(F32)
    mid = r.astype(BF16)
    lo = (r - mid.astype(F32)).astype(BF16)
    return hi, mid, lo


def _const_spec(shape):
    zeros = (0,) * len(shape)
    return pl.BlockSpec(shape, lambda *_: zeros, pipeline_mode=pl.Buffered(1))


def _ffn_kernel(x_ref, gpre_ref, wg_ref, wu_ref, wd_ref, gpost_ref, o_ref, *, parts):
    rows = x_ref.shape[0] // parts
    sl = [slice(p * rows, (p + 1) * rows) for p in range(parts)]
    xn = [_rms(x_ref[s, :], gpre_ref[...]).astype(BF16) for s in sl]
    act = []
    for p in range(parts):
        gate = _dot(xn[p], wg_ref[:, 0:D_FF])
        up = _dot(xn[p], wu_ref[:, 0:D_FF])
        act.append((_silu(gate) * up).astype(BF16))
    for p in range(parts):
        y = _dot(act[p], wd_ref[:, 0:D_MODEL])
        o_ref[sl[p], :] = x_ref[sl[p], :] + 0.5 * _rms(y, gpost_ref[...])


def _ffn(x2d, gpre, wg, wu, wd, gpost, tm):
    m = x2d.shape[0]
    assert m % tm == 0
    parts = max(1, tm // FFN_PART_ROWS)
    return pl.pallas_call(
        functools.partial(_ffn_kernel, parts=parts),
        grid=(m // tm,),
        in_specs=[
            pl.BlockSpec((tm, D_MODEL), lambda i: (i, 0)),
            _const_spec(gpre.shape), _const_spec(wg.shape), _const_spec(wu.shape),
            _const_spec(wd.shape), _const_spec(gpost.shape),
        ],
        out_specs=pl.BlockSpec((tm, D_MODEL), lambda i: (i, 0)),
        out_shape=jax.ShapeDtypeStruct((m, D_MODEL), F32),
        compiler_params=pltpu.CompilerParams(
            dimension_semantics=("arbitrary",), vmem_limit_bytes=VMEM_LIMIT_BYTES),
        name="ffn_half_step",
    )(x2d, gpre, wg, wu, wd, gpost)


def _rope(x, cos_t, sa_t, sb_t):
    w = x.shape[1]
    reps = w // LANES
    cos_f = jnp.tile(cos_t, (1, reps))
    sa_f = jnp.tile(sa_t, (1, reps))
    sb_f = jnp.tile(sb_t, (1, reps))
    half = ROT_DIM // 2
    return x * cos_f + pltpu.roll(x, w - half, 1) * sa_f + pltpu.roll(x, half, 1) * sb_f


def _transpose_rows_padded(x):
    r, c = x.shape
    rp = -(-r // LANES) * LANES
    cp = -(-c // LANES) * LANES
    if cp != c:
        x = jnp.concatenate([x, jnp.zeros((r, cp - c), x.dtype)], axis=1)
    if rp != r:
        x = jnp.concatenate([x, jnp.zeros((rp - r, cp), x.dtype)], axis=0)
    xt = x.T
    return xt if (rp == r and cp == c) else xt[:c, :r]


def _mixer_kernel(*refs, tm, q, qa, n_keep, has_state):
    it = iter(refs)
    x_ref = next(it)
    cos_ref, sa_ref, sb_ref = next(it), next(it), next(it)
    if has_state:
        conv0_ref, ssm0_ref, ck_ref, cv_ref = next(it), next(it), next(it), next(it)
    (gpre_ref, wxbc_ref, wdt_ref, wq_ref, wk_ref, wv_ref,
     convw_ref, convb_ref, dtb_ref, arow_ref, dsk_ref, expand_ref, sink_ref,
     wz_ref, wgm_ref, wga_ref, mng_ref, wbm_ref, wba_ref, wo_ref, gpost_ref) = [
         next(it) for _ in range(21)]
    y_ref, convo_ref, ssmo_ref, ko_ref, vo_ref = [next(it) for _ in range(5)]
    (cbuf_s, xbd_s, c_s, bt_s, acs_s, acst2_s, dtt2_s, eax_s, xw_s, decx_s,
     q_s, kwin_s, vwt_s, ht_s, xn_s, ym_s, at_s, zs_s, gm_s, ga_s) = [next(it) for _ in range(20)]

    si = pl.program_id(1)
    ns = pl.num_programs(1)
    nch = tm // q
    cpb = qa // q
    w2 = 2 * q
    wlen = WINDOW + qa
    assert nch <= DEC_ROWS

    @pl.when(si == 0)
    def _():
        if has_state:
            cbuf_s[CONV_PAD - (CONV_W - 1):CONV_PAD, :] = conv0_ref[0]
            for h in range(M_HEADS):
                g, r = divmod(h, M_HPG)
                ht_s[g, :, r * M_HEAD_DIM:(r + 1) * M_HEAD_DIM] = _transpose_rows_padded(ssm0_ref[0, h])
            kwin_s[0:WINDOW, :] = ck_ref[0].astype(BF16)
            vwt_s[:, 0:WINDOW] = _transpose_rows_padded(cv_ref[0]).astype(BF16)
        else:
            cbuf_s[0:CONV_PAD, :] = jnp.zeros((CONV_PAD, CONV_DIM), F32)
            ht_s[...] = jnp.zeros(ht_s.shape, F32)
            kwin_s[0:WINDOW, :] = jnp.zeros((WINDOW, KV_WIDTH), BF16)
            vwt_s[:, 0:WINDOW] = jnp.zeros((KV_WIDTH, WINDOW), BF16)

    x = x_ref[...]
    xn = _rms(x, gpre_ref[...]).astype(BF16)
    xn_s[...] = xn

    cos_t, sa_t, sb_t = cos_ref[...], sa_ref[...], sb_ref[...]
    lane_b = lax.broadcasted_iota(jnp.int32, (tm, COL_BLK), 1)
    even_head = (lane_b % PAIR_W) < M_HEAD_DIM
    d0 = 2 * tm
    n_xb = D_INNER // COL_BLK
    n_bb = BC_W // COL_BLK
    st = {}

    def dt_cumsum():
        dt = _softplus(_dot(xn, wdt_ref[...]) + dtb_ref[...])
        rr = lax.broadcasted_iota(jnp.int32, (tm, tm), 0)
        cc = lax.broadcasted_iota(jnp.int32, (tm, tm), 1)
        tri = jnp.where((rr >= cc) & ((rr // q) == (cc // q)), 1.0, 0.0).astype(BF16)
        cs3 = _dot(tri, jnp.concatenate(_split3(dt * arow_ref[...]), axis=1))
        acs = cs3[:, 0:HP] + cs3[:, HP:2 * HP] + cs3[:, 2 * HP:3 * HP]
        acs_s[...] = acs
        st["dt"], st["acs"] = dt, acs

    def dt_factors():
        dt, acs = st["dt"], st["acs"]
        lasts = [acs[c * q + q - 1:c * q + q, :] for c in range(nch)]
        last_b = jnp.concatenate([jnp.broadcast_to(l, (q, HP)) for l in lasts], axis=0)
        eacs = jnp.exp2(acs)
        wend = dt * jnp.exp2(last_b - acs)
        dec = jnp.exp2(jnp.concatenate(lasts + [jnp.zeros((DEC_ROWS - nch, HP), F32)], axis=0))
        st["lhs"] = jnp.concatenate([eacs.astype(BF16), wend.astype(BF16), *_split3(dec)], axis=0)

    def pair_rows(name, dst):
        a = st[name]
        nxt = pltpu.roll(a, HP - 1, 1)
        z = jnp.concatenate(
            [p[c * q:(c + 1) * q] for c in range(nch) for p in (a, nxt)], axis=0)
        zt = _transpose_rows_padded(z)
        for c in range(nch):
            dst[c] = zt[:, c * w2:(c + 1) * w2]

    def proj(jb):
        cols = slice(jb * COL_BLK, (jb + 1) * COL_BLK)
        cbuf_s[CONV_PAD:CONV_PAD + tm, cols] = _dot(xn, wxbc_ref[:, cols])

    def conv(jb):
        cols = slice(jb * COL_BLK, (jb + 1) * COL_BLK)
        acc = convb_ref[:, cols] + (cbuf_s[CONV_PAD:CONV_PAD + tm, cols]
                                    * convw_ref[CONV_W - 1:CONV_W, cols])
        for j in range(CONV_W - 1):
            off = CONV_PAD - (CONV_W - 1) + j
            acc = acc + cbuf_s[off:off + tm, cols] * convw_ref[j:j + 1, cols]
        return _silu(acc)

    def x_block(jb):
        cols = slice(jb * COL_BLK, (jb + 1) * COL_BLK)
        xc = conv(jb)
        xe = jnp.where(even_head, xc, 0.0).astype(BF16)
        xo = jnp.where(even_head, 0.0, xc).astype(BF16)
        for c in range(nch):
            xbd_s[c, 0:q, cols] = xe[c * q:(c + 1) * q]
            xbd_s[c, q:w2, cols] = xo[c * q:(c + 1) * q]
        ex = _dot(st["lhs"], expand_ref[:, cols])
        eax_s[:, cols] = ex[0:tm].astype(BF16)
        xw_s[:, cols] = (xc * ex[tm:d0]).astype(BF16)
        decx_s[:, cols] = (ex[d0:d0 + DEC_ROWS] + ex[d0 + DEC_ROWS:d0 + 2 * DEC_ROWS]
                           + ex[d0 + 2 * DEC_ROWS:d0 + 3 * DEC_ROWS])

    def b_block(jb):
        btt = _transpose_rows_padded(conv(jb))
        r = slice((jb - n_xb) * COL_BLK, (jb - n_xb + 1) * COL_BLK)
        for c in range(nch):
            bt_s[c, r, :] = btt[:, c * q:(c + 1) * q].astype(BF16)

    def c_block(jb):
        r = slice((jb - n_xb - n_bb) * COL_BLK, (jb - n_xb - n_bb + 1) * COL_BLK)
        c_s[:, r] = conv(jb).astype(BF16)

    def q_block(jq):
        cols = slice(jq * COL_BLK, (jq + 1) * COL_BLK)
        qv = _rope(_dot(xn, wq_ref[:, cols]), cos_t, sa_t, sb_t) * (LOG2E * A_HEAD_DIM ** -0.5)
        per = COL_BLK // A_HEAD_DIM
        for hh in range(per):
            q_s[jq * per + hh] = qv[:, hh * A_HEAD_DIM:(hh + 1) * A_HEAD_DIM].astype(BF16)

    def kv_block():
        kv = _rope(_dot(xn, wk_ref[...]), cos_t, sa_t, sb_t)
        vv = _dot(xn, wv_ref[...])
        kwin_s[WINDOW:WINDOW + tm, :] = kv.astype(BF16)
        vwt_s[:, WINDOW:WINDOW + tm] = _transpose_rows_padded(vv).astype(BF16)
        ko_ref[0] = kv[tm - n_keep:, :]
        vo_ref[0] = vv[tm - n_keep:, :]

    P = functools.partial
    bc = list(range(n_xb, n_xb + 2 * n_bb))
    fin = [P(b_block, j) if j < n_xb + n_bb else P(c_block, j) for j in bc]
    light = [dt_cumsum, dt_factors, P(pair_rows, "acs", acst2_s), P(pair_rows, "dt", dtt2_s)]
    light += [P(q_block, jq) for jq in range(A_WIDTH // COL_BLK)] + [kv_block]
    order = [P(proj, bc[0])]
    blocks = bc + list(range(n_xb))
    fins = fin + [P(x_block, j) for j in range(n_xb)]
    for i in range(len(blocks)):
        if i + 1 < len(blocks):
            order.append(P(proj, blocks[i + 1]))
        order.append(fins[i])
        if light:
            order.append(light.pop(0))
    order += light
    for stage in order:
        stage()
    tail = cbuf_s[CONV_PAD + tm - (CONV_W - 1):CONV_PAD + tm, :]
    convo_ref[0] = tail
    cbuf_s[CONV_PAD - (CONV_W - 1):CONV_PAD, :] = tail

    row2 = lax.broadcasted_iota(jnp.int32, (q, w2), 0)
    lane2 = lax.broadcasted_iota(jnp.int32, (q, w2), 1)
    first_half = lane2 < q
    pos2 = jnp.where(first_half, lane2, lane2 - q)
    row_g = lax.broadcasted_iota(jnp.int32, (q, PAIRS_PER_GROUP * w2), 0)
    pos_g = lax.broadcasted_iota(jnp.int32, (q, PAIRS_PER_GROUP * w2), 1) % q
    causal_g = row_g >= pos_g
    diag_g = row_g == pos_g
    if not has_state:
        krow = lax.broadcasted_iota(jnp.int32, (wlen, qa), 0)
        qcol = lax.broadcasted_iota(jnp.int32, (wlen, qa), 1)
        dchunk = krow // CHUNK - qcol // CHUNK
        band = (dchunk >= 0) & (dchunk <= WINDOW // CHUNK)
    sink_rows = [
        jnp.concatenate([jnp.full((1, qa), sink_ref[j * A_REP + r], F32) for r in range(A_REP)],
                        axis=1) for j in range(A_KV)]

    def ssd_group(c, g):
        rows = slice(c * q, (c + 1) * q)
        acs_c = acs_s[rows, :]
        if True:
            bt_g = bt_s[c, g * D_STATE:(g + 1) * D_STATE, :]
            c_g = c_s[rows, g * D_STATE:(g + 1) * D_STATE]
            cb = _dot(c_g, bt_g)
            cb2 = jnp.concatenate([cb, cb], axis=1)
            ystate = _dot(c_g, ht_s[g].astype(BF16))
            gl = slice(g * GW, (g + 1) * GW)
            pairs = range(g * PAIRS_PER_GROUP, (g + 1) * PAIRS_PER_GROUP)
            col = jnp.concatenate(
                [jnp.where(first_half, acs_c[:, 2 * k:2 * k + 1], acs_c[:, 2 * k + 1:2 * k + 2])
                 for k in pairs], axis=1)
            arow = jnp.concatenate([acst2_s[c, 2 * k:2 * k + 1, :] for k in pairs], axis=1)
            drow = jnp.concatenate([dtt2_s[c, 2 * k:2 * k + 1, :] for k in pairs], axis=1)
            dskw = jnp.concatenate([dsk_ref[k:k + 1, 0:w2] for k in pairs], axis=1)
            seg = jnp.where(causal_g, jnp.exp2(col - arow), 0.0)
            m2 = seg * (jnp.concatenate([cb2] * PAIRS_PER_GROUP, axis=1) * drow)
            m2 = (m2 + jnp.where(diag_g, dskw, 0.0)).astype(BF16)
            y2 = jnp.concatenate(
                [_dot(m2[:, kk * w2:(kk + 1) * w2], xbd_s[c, :, k * PAIR_W:(k + 1) * PAIR_W])
                 for kk, k in enumerate(pairs)], axis=1)
            ym_s[rows, gl] = (y2 + ystate * eax_s[rows, gl].astype(F32)).astype(BF16)
            ht_s[g] = ht_s[g] * decx_s[c:c + 1, gl] + _dot(bt_g, xw_s[rows, gl])

    def attn_head(i, j):
        ra = i * qa
        rows = slice(ra, ra + qa)
        if not has_state:
            ok = band if ra >= WINDOW else band & ((krow >= WINDOW - ra) | (si > 0))
            bias = jnp.where(ok, 0.0, -jnp.inf)
            bias4 = jnp.concatenate([bias] * A_REP, axis=1)
        hs = slice(j * A_HEAD_DIM, (j + 1) * A_HEAD_DIM)
        qst = q_s[j * A_REP:(j + 1) * A_REP, rows, :].reshape(A_REP * qa, A_HEAD_DIM)
        st = _dot_nt(kwin_s[ra:ra + wlen, hs], qst)
        if not has_state:
            st = st + bias4
        m = jnp.maximum(jnp.max(st, axis=0, keepdims=True), sink_rows[j])
        e = jnp.exp2(st - m)
        den = jnp.sum(e, axis=0, keepdims=True) + jnp.exp2(sink_rows[j] - m)
        ot = _dot(vwt_s[hs, ra:ra + wlen], e.astype(BF16)) * (1.0 / den)
        for r2 in range(A_REP // 2):
            two = jnp.concatenate(
                [ot[:, (2 * r2) * qa:(2 * r2 + 1) * qa], ot[:, (2 * r2 + 1) * qa:(2 * r2 + 2) * qa]],
                axis=0)
            h0 = j * A_REP + 2 * r2
            at_s[rows, h0 * A_HEAD_DIM:(h0 + 2) * A_HEAD_DIM] = (
                _transpose_rows_padded(two).astype(BF16))

    def z_block(jb):
        cols = slice(jb * SIDE_BLK, (jb + 1) * SIDE_BLK)
        zs_s[:, cols] = _silu(_dot(xn_s[...], wz_ref[:, cols])).astype(BF16)

    def gate_block(w_ref, dst, jb):
        cols = slice(jb * SIDE_BLK, (jb + 1) * SIDE_BLK)
        dst[:, cols] = _sigmoid(_dot(xn_s[...], w_ref[:, cols])).astype(BF16)

    side_work = [functools.partial(z_block, jb) for jb in range(D_INNER // SIDE_BLK)]
    side_work += [functools.partial(gate_block, wgm_ref, gm_s, jb) for jb in range(D_MODEL // SIDE_BLK)]
    side_work += [functools.partial(gate_block, wga_ref, ga_s, jb) for jb in range(D_MODEL // SIDE_BLK)]

    def out_norm(i):
        rows = slice(i * qa, (i + 1) * qa)
        ym_s[rows, :] = _rms(ym_s[rows, :].astype(F32) * zs_s[rows, :].astype(F32),
                             mng_ref[...]).astype(BF16)

    def out_mix(i):
        rows = slice(i * qa, (i + 1) * qa)
        br_m = _dot(ym_s[rows, :], wbm_ref[:, 0:D_MODEL])
        br_a = _dot(at_s[rows, :], wba_ref[:, 0:D_MODEL])
        gm_s[rows, :] = (gm_s[rows, :].astype(F32) * br_m
                         + ga_s[rows, :].astype(F32) * br_a).astype(BF16)

    def out_proj(i):
        rows = slice(i * qa, (i + 1) * qa)
        y_ref[rows, :] = x_ref[rows, :] + _rms(_dot(gm_s[rows, :], wo_ref[:, 0:D_MODEL]),
                                               gpost_ref[...])

    scan = [functools.partial(ssd_group, c, g) for c in range(nch) for g in range(M_GROUPS)]
    heads = [functools.partial(attn_head, i, j) for i in range(tm // qa) for j in range(A_KV)]
    units = []
    for k, unit in enumerate(scan):
        units.append(unit)
        units += heads[min(len(heads), 2 * len(heads) * k // len(scan)):
                       min(len(heads), 2 * len(heads) * (k + 1) // len(scan))]
    done = 0
    for u, unit in enumerate(units):
        unit()
        upto = -(-len(side_work) * (u + 1) // len(units))
        for side in side_work[done:upto]:
            side()
        done = upto
    for i in range(tm // qa):
        out_norm(i)
    for i in range(tm // qa):
        out_mix(i)
    for i in range(tm // qa):
        out_proj(i)

    if not has_state:
        kwin_s[0:WINDOW, :] = kwin_s[tm:tm + WINDOW, :]
        vwt_s[:, 0:WINDOW] = vwt_s[:, tm:tm + WINDOW]

    @pl.when(si == ns - 1)
    def _():
        for h in range(M_HEADS):
            g, r = divmod(h, M_HPG)
            ssmo_ref[0, h] = _transpose_rows_padded(ht_s[g, :, r * M_HEAD_DIM:(r + 1) * M_HEAD_DIM])


def _mixer(x3d, tables, state, wts, *, tm, q, qa, n_keep):
    bsz, seq, _ = x3d.shape
    assert seq % tm == 0 and tm % qa == 0 and qa % q == 0 and q % BF16_ROWS == 0
    assert tm >= CONV_PAD >= CONV_W - 1
    ns = seq // tm
    nch = tm // q
    has_state = state is not None
    if has_state:
        assert ns == 1 and n_keep == tm and qa == tm
    else:
        assert tm >= WINDOW and n_keep == WINDOW and seq >= WINDOW and q == CHUNK and qa % CHUNK == 0

    in_specs = [pl.BlockSpec((None, tm, D_MODEL), lambda b, s: (b, s, 0))]
    in_specs += [pl.BlockSpec((tm, LANES), lambda b, s: (s, 0))] * 3
    args = [x3d, *tables]
    if has_state:
        conv0, ssm0, ck, cv = state
        assert ck.shape[1] == WINDOW
        in_specs += [
            pl.BlockSpec((1, CONV_W - 1, CONV_DIM), lambda b, s: (b, 0, 0)),
            pl.BlockSpec((1, M_HEADS, M_HEAD_DIM, D_STATE), lambda b, s: (b, 0, 0, 0)),
            pl.BlockSpec((1, WINDOW, KV_WIDTH), lambda b, s: (b, 0, 0)),
            pl.BlockSpec((1, WINDOW, KV_WIDTH), lambda b, s: (b, 0, 0)),
        ]
        args += [conv0, ssm0, ck, cv]
    for name, w in wts:
        if name == "sink":
            in_specs.append(pl.BlockSpec(memory_space=pltpu.SMEM))
        else:
            in_specs.append(_const_spec(w.shape))
        args.append(w)

    out_shape = [
        jax.ShapeDtypeStruct((bsz, seq, D_MODEL), F32),
        jax.ShapeDtypeStruct((bsz, CONV_W - 1, CONV_DIM), F32),
        jax.ShapeDtypeStruct((bsz, M_HEADS, M_HEAD_DIM, D_STATE), F32),
        jax.ShapeDtypeStruct((bsz, n_keep, KV_WIDTH), F32),
        jax.ShapeDtypeStruct((bsz, n_keep, KV_WIDTH), F32),
    ]
    out_specs = [
        pl.BlockSpec((None, tm, D_MODEL), lambda b, s: (b, s, 0)),
        pl.BlockSpec((1, CONV_W - 1, CONV_DIM), lambda b, s: (b, 0, 0)),
        pl.BlockSpec((1, M_HEADS, M_HEAD_DIM, D_STATE), lambda b, s: (b, 0, 0, 0)),
        pl.BlockSpec((1, n_keep, KV_WIDTH), lambda b, s: (b, 0, 0)),
        pl.BlockSpec((1, n_keep, KV_WIDTH), lambda b, s: (b, 0, 0)),
    ]
    w2 = 2 * q
    scratch = [
        pltpu.VMEM((CONV_PAD + tm, CONV_DIM), F32),
        pltpu.VMEM((nch, w2, D_INNER), BF16),
        pltpu.VMEM((tm, BC_W), BF16),
        pltpu.VMEM((nch, BC_W, q), BF16),
        pltpu.VMEM((tm, HP), F32),
        pltpu.VMEM((nch, HP, w2), F32),
        pltpu.VMEM((nch, HP, w2), F32),
        pltpu.VMEM((tm, D_INNER), BF16),
        pltpu.VMEM((tm, D_INNER), BF16),
        pltpu.VMEM((DEC_ROWS, D_INNER), F32),
        pltpu.VMEM((A_HEADS, tm, A_HEAD_DIM), BF16),
        pltpu.VMEM((WINDOW + tm, KV_WIDTH), BF16),
        pltpu.VMEM((KV_WIDTH, WINDOW + tm), BF16),
        pltpu.VMEM((M_GROUPS, D_STATE, GW), F32),
        pltpu.VMEM((tm, D_MODEL), BF16),
        pltpu.VMEM((tm, D_INNER), BF16),
        pltpu.VMEM((tm, A_WIDTH), BF16),
        pltpu.VMEM((tm, D_INNER), BF16),
        pltpu.VMEM((tm, D_MODEL), BF16),
        pltpu.VMEM((tm, D_MODEL), BF16),
    ]
    kern = functools.partial(_mixer_kernel, tm=tm, q=q, qa=qa, n_keep=n_keep, has_state=has_state)
    return pl.pallas_call(
        kern,
        grid=(bsz, ns),
        in_specs=in_specs,
        out_specs=out_specs,
        out_shape=out_shape,
        scratch_shapes=scratch,
        compiler_params=pltpu.CompilerParams(
            dimension_semantics=("arbitrary", "arbitrary"), vmem_limit_bytes=VMEM_LIMIT_BYTES),
        name="mixer_state" if has_state else "mixer_prompt",
    )(*args)


def _rope_tables(pos):
    half = ROT_DIM // 2
    inv_freq = ROPE_THETA ** (-jnp.arange(0, ROT_DIM, 2, dtype=F32) / ROT_DIM)
    ang = pos.astype(F32)[:, None] * inv_freq[None, :]
    cos, sin = jnp.cos(ang), jnp.sin(ang)
    n = pos.shape[0]
    ones = jnp.ones((n, A_HEAD_DIM - ROT_DIM), F32)
    zeros = jnp.zeros((n, A_HEAD_DIM - ROT_DIM), F32)
    zh = jnp.zeros((n, half), F32)
    cos_t = jnp.concatenate([cos, cos, ones], axis=1)
    sa_t = jnp.concatenate([-sin, zh, zeros], axis=1)
    sb_t = jnp.concatenate([zh, sin, zeros], axis=1)
    rep = LANES // A_HEAD_DIM
    return tuple(jnp.tile(t, (1, rep)) for t in (cos_t, sa_t, sb_t))


def _matmul_weight(w):
    w = w.astype(BF16)
    if (w.shape[1] // LANES) % 4 == 0:
        w = jnp.pad(w, ((0, 0), (0, LANES)))
    return w


def _pick_tile(n, pref):
    t = min(n, pref)
    while n % t:
        t //= 2
    return t


def kernel(x_prompt, x_sample, state_conv, state_ssm, cache_k, cache_v, ffn1_pre_g, ffn1_w_gu, ffn1_w_down, ffn1_post_g, mix_pre_g, w_in, conv_w, conv_b, dt_bias, a_log, d_skip, m_norm_g, attn_sink, w_br_m, w_br_a, w_o, mix_post_g, ffn2_pre_g, ffn2_w_gu, ffn2_w_down, ffn2_post_g):
    depth = w_in.shape[0]
    bp, lp, _ = x_prompt.shape
    bs, ls, _ = x_sample.shape
    pos_p = jnp.arange(lp, dtype=jnp.int32)
    pos_s = PAST_LEN + jnp.arange(ls, dtype=jnp.int32)
    tab_p = _rope_tables(pos_p)
    tab_s = _rope_tables(pos_s)
    expand = (jnp.arange(HP)[:, None] == (jnp.arange(D_INNER)[None, :] // M_HEAD_DIM)).astype(BF16)

    yp, ys = x_prompt, x_sample
    new_p, new_s = [], []
    for l in range(depth):
        def row(v):
            return v[l].reshape(1, -1).astype(F32)

        def ffn_w(w_gu, w_down):
            return (_matmul_weight(w_gu[l][:, :D_FF]), _matmul_weight(w_gu[l][:, D_FF:]),
                    _matmul_weight(w_down[l]))

        f1 = ffn_w(ffn1_w_gu, ffn1_w_down)
        f2 = ffn_w(ffn2_w_gu, ffn2_w_down)
        splits = np.cumsum(IN_SIZES)[:-1]
        wz, wxbc, wdt, wq, wk, wv, wgm, wga = jnp.split(w_in[l], splits, axis=1)
        pad_h = HP - M_HEADS
        wdt = jnp.pad(wdt, ((0, 0), (0, pad_h)))
        dtb = jnp.pad(dt_bias[l].astype(F32), (0, pad_h)).reshape(1, HP)
        arow = jnp.pad(-jnp.exp(a_log[l].astype(F32)) * LOG2E, (0, pad_h)).reshape(1, HP)
        def dsk_for(q):
            return jnp.repeat(d_skip[l].astype(F32).reshape(N_PAIRS, 2), q, axis=1)

        def wts_for(q):
            return [
                ("gpre", row(mix_pre_g)),
                ("wxbc", _matmul_weight(wxbc)), ("wdt", wdt.astype(BF16)),
                ("wq", _matmul_weight(wq)), ("wk", wk.astype(BF16)), ("wv", wv.astype(BF16)),
                ("convw", conv_w[l].astype(F32)), ("convb", row(conv_b)),
                ("dtb", dtb), ("arow", arow), ("dsk", dsk_for(q)), ("expand", _matmul_weight(expand)),
                ("sink", attn_sink[l].astype(F32) * LOG2E),
                ("wz", _matmul_weight(wz)), ("wgm", _matmul_weight(wgm)), ("wga", _matmul_weight(wga)),
                ("mng", row(m_norm_g)), ("wbm", _matmul_weight(w_br_m[l])),
                ("wba", _matmul_weight(w_br_a[l])), ("wo", _matmul_weight(w_o[l])),
                ("gpost", row(mix_post_g)),
            ]

        def layer(x3d, tables, state, tm_ffn, tm_mix, q, qa, n_keep):
            b, s, _ = x3d.shape
            x2d = x3d.reshape(b * s, D_MODEL)
            x2d = _ffn(x2d, row(ffn1_pre_g), *f1, row(ffn1_post_g), _pick_tile(b * s, tm_ffn))
            y3d, conv_o, ssm_o, k_o, v_o = _mixer(
                x2d.reshape(b, s, D_MODEL), tables, state, wts_for(q),
                tm=tm_mix, q=q, qa=qa, n_keep=n_keep)
            y2d = _ffn(y3d.reshape(b * s, D_MODEL), row(ffn2_pre_g), *f2, row(ffn2_post_g),
                       _pick_tile(b * s, tm_ffn))
            return (y2d.reshape(b, s, D_MODEL), conv_o, ssm_o,
                    k_o.reshape(b, n_keep, A_KV, A_HEAD_DIM), v_o.reshape(b, n_keep, A_KV, A_HEAD_DIM))

        rows_p = min(WINDOW, lp)
        yp, *st_p = layer(yp, tab_p, None, 1024, _pick_tile(lp, 256), CHUNK, 2 * CHUNK, rows_p)
        n_cache = cache_k.shape[2]
        def of_layer(a):
            return a.reshape(a.shape[1:]) if depth == 1 else a[l]

        state = (of_layer(state_conv).astype(F32), of_layer(state_ssm).astype(F32),
                 of_layer(cache_k).reshape(bs, n_cache, KV_WIDTH),
                 of_layer(cache_v).reshape(bs, n_cache, KV_WIDTH))
        q_s = min(ls, CHUNK)
        ys, *st_s = layer(ys, tab_s, state, 256, ls, q_s, ls, ls)
        new_p.append(st_p)
        new_s.append(st_s)

    def per_layer(t):
        return t[0][None] if depth == 1 else jnp.stack(t)

    conv_p, ssm_p, k_p, v_p = [per_layer(t) for t in zip(*new_p)]
    conv_s, ssm_s, k_s, v_s = [per_layer(t) for t in zip(*new_s)]
    return (yp, ys, conv_p, ssm_p, k_p, v_p, conv_s, ssm_s, k_s, v_s)
```

```python
import functools

import jax
import jax.numpy as jnp
import numpy as np
from jax import lax
from jax.experimental import pallas as pl
from jax.experimental.pallas import tpu as pltpu

D_MODEL = 1024
CHUNK = 64
EPS = 1e-6
D_INNER = 2 * D_MODEL
M_HEAD_DIM = 64
M_HEADS = D_INNER // M_HEAD_DIM
M_GROUPS = 4
M_HPG = M_HEADS // M_GROUPS
D_STATE = 128
CONV_W = 4
CONV_DIM = D_INNER + 2 * M_GROUPS * D_STATE
A_HEADS = 16
A_KV = 4
A_REP = A_HEADS // A_KV
A_HEAD_DIM = 64
A_WIDTH = A_HEADS * A_HEAD_DIM
KV_WIDTH = A_KV * A_HEAD_DIM
WINDOW = 128
ROT_DIM = A_HEAD_DIM // 4
ROPE_THETA = 500000.0
LOG2E = 1.4426950408889634
D_FF = 2816
PAST_LEN = 1024
IN_SIZES = (D_INNER, CONV_DIM, M_HEADS, A_WIDTH, KV_WIDTH, KV_WIDTH, D_MODEL, D_MODEL)

LANES = 128
SUBLANES = 8
BF16_ROWS = 16
VMEM_LIMIT_BYTES = 60 * 1024 * 1024

HP = LANES
GW = M_HPG * M_HEAD_DIM
BC_W = M_GROUPS * D_STATE
PAIR_W = 2 * M_HEAD_DIM
N_PAIRS = M_HEADS // 2
PAIRS_PER_GROUP = M_HPG // 2
CONV_PAD = SUBLANES
COL_BLK = 256
SIDE_BLK = 256
FFN_PART_ROWS = 256
DEC_ROWS = BF16_ROWS

BF16 = jnp.bfloat16
F32 = jnp.float32


def _dot(a, b):
    return jnp.dot(a, b, preferred_element_type=F32)


def _dot_nt(a, b):
    return lax.dot_general(a, b, (((1,), (1,)), ((), ())), preferred_element_type=F32)


def _rms(x, g):
    return x * lax.rsqrt(jnp.mean(x * x, axis=-1, keepdims=True) + EPS) * g


def _silu(x):
    h = 0.5 * x
    return h + h * jnp.tanh(h)


def _sigmoid(x):
    return 1.0 / (1.0 + jnp.exp(-x))


def _softplus(x):
    return jnp.maximum(x, 0.0) + jnp.log(1.0 + jnp.exp(-jnp.abs(x)))


def _split3(x):
    hi = x.astype(BF16)
    r = x - hi.astype(F32)
    mid = r.astype(BF16)
    lo = (r - mid.astype(F32)).astype(BF16)
    return hi, mid, lo


def _const_spec(shape):
    zeros = (0,) * len(shape)
    return pl.BlockSpec(shape, lambda *_: zeros, pipeline_mode=pl.Buffered(1))


def _ffn_kernel(x_ref, gpre_ref, wg_ref, wu_ref, wd_ref, gpost_ref, o_ref, *, parts):
    rows = x_ref.shape[0] // parts
    sl = [slice(p * rows, (p + 1) * rows) for p in range(parts)]
    xn = [_rms(x_ref[s, :], gpre_ref[...]).astype(BF16) for s in sl]
    act = []
    for p in range(parts):
        gate = _dot(xn[p], wg_ref[:, 0:D_FF])
        up = _dot(xn[p], wu_ref[:, 0:D_FF])
        act.append((_silu(gate) * up).astype(BF16))
    for p in range(parts):
        y = _dot(act[p], wd_ref[:, 0:D_MODEL])
        o_ref[sl[p], :] = x_ref[sl[p], :] + 0.5 * _rms(y, gpost_ref[...])


def _ffn(x2d, gpre, wg, wu, wd, gpost, tm):
    m = x2d.shape[0]
    assert m % tm == 0
    parts = max(1, tm // FFN_PART_ROWS)
    return pl.pallas_call(
        functools.partial(_ffn_kernel, parts=parts),
        grid=(m // tm,),
        in_specs=[
            pl.BlockSpec((tm, D_MODEL), lambda i: (i, 0)),
            _const_spec(gpre.shape), _const_spec(wg.shape), _const_spec(wu.shape),
            _const_spec(wd.shape), _const_spec(gpost.shape),
        ],
        out_specs=pl.BlockSpec((tm, D_MODEL), lambda i: (i, 0)),
        out_shape=jax.ShapeDtypeStruct((m, D_MODEL), F32),
        compiler_params=pltpu.CompilerParams(
            dimension_semantics=("arbitrary",), vmem_limit_bytes=VMEM_LIMIT_BYTES),
        name="ffn_half_step",
    )(x2d, gpre, wg, wu, wd, gpost)


def _rope(x, cos_t, sa_t, sb_t):
    w = x.shape[1]
    reps = w // LANES
    cos_f = jnp.tile(cos_t, (1, reps))
    sa_f = jnp.tile(sa_t, (1, reps))
    sb_f = jnp.tile(sb_t, (1, reps))
    half = ROT_DIM // 2
    return x * cos_f + pltpu.roll(x, w - half, 1) * sa_f + pltpu.roll(x, half, 1) * sb_f


def _transpose_rows_padded(x):
    r, c = x.shape
    rp = -(-r // LANES) * LANES
    cp = -(-c // LANES) * LANES
    if cp != c:
        x = jnp.concatenate([x, jnp.zeros((r, cp - c), x.dtype)], axis=1)
    if rp != r:
        x = jnp.concatenate([x, jnp.zeros((rp - r, cp), x.dtype)], axis=0)
    xt = x.T
    return xt if (rp == r and cp == c) else xt[:c, :r]


def _mixer_kernel(*refs, tm, q, qa, n_keep, has_state):
    it = iter(refs)
    x_ref = next(it)
    cos_ref, sa_ref, sb_ref = next(it), next(it), next(it)
    if has_state:
        conv0_ref, ssm0_ref, ck_ref, cv_ref = next(it), next(it), next(it), next(it)
    (gpre_ref, wxbc_ref, wdt_ref, wq_ref, wk_ref, wv_ref,
     convw_ref, convb_ref, dtb_ref, arow_ref, dsk_ref, expand_ref, sink_ref,
     wz_ref, wgm_ref, wga_ref, mng_ref, wbm_ref, wba_ref, wo_ref, gpost_ref) = [
         next(it) for _ in range(21)]
    y_ref, convo_ref, ssmo_ref, ko_ref, vo_ref = [next(it) for _ in range(5)]
    (cbuf_s, xbd_s, c_s, bt_s, acs_s, acst2_s, dtt2_s, eax_s, xw_s, decx_s,
     q_s, kwin_s, vwt_s, ht_s, xn_s, ym_s, at_s, zs_s, gm_s, ga_s) = [next(it) for _ in range(20)]

    si = pl.program_id(1)
    ns = pl.num_programs(1)
    nch = tm // q
    w2 = 2 * q
    wlen = WINDOW + qa
    assert nch <= DEC_ROWS

    @pl.when(si == 0)
    def _():
        if has_state:
            cbuf_s[CONV_PAD - (CONV_W - 1):CONV_PAD, :] = conv0_ref[0]
            for h in range(M_HEADS):
                g, r = divmod(h, M_HPG)
                ht_s[g, :, r * M_HEAD_DIM:(r + 1) * M_HEAD_DIM] = _transpose_rows_padded(ssm0_ref[0, h])
            kwin_s[0:WINDOW, :] = ck_ref[0].astype(BF16)
            vwt_s[:, 0:WINDOW] = _transpose_rows_padded(cv_ref[0]).astype(BF16)
        else:
            cbuf_s[0:CONV_PAD, :] = jnp.zeros((CONV_PAD, CONV_DIM), F32)
            ht_s[...] = jnp.zeros(ht_s.shape, F32)
            kwin_s[0:WINDOW, :] = jnp.zeros((WINDOW, KV_WIDTH), BF16)
            vwt_s[:, 0:WINDOW] = jnp.zeros((KV_WIDTH, WINDOW), BF16)

    x = x_ref[...]
    xn = _rms(x, gpre_ref[...]).astype(BF16)
    xn_s[...] = xn

    cos_t, sa_t, sb_t = cos_ref[...], sa_ref[...], sb_ref[...]
    lane_b = lax.broadcasted_iota(jnp.int32, (tm, COL_BLK), 1)
    even_head = (lane_b % PAIR_W) < M_HEAD_DIM
    d0 = 2 * tm
    n_xb = D_INNER // COL_BLK
    n_bb = BC_W // COL_BLK
    st = {}

    def dt_cumsum():
        dt = _softplus(_dot(xn, wdt_ref[...]) + dtb_ref[...])
        rr = lax.broadcasted_iota(jnp.int32, (tm, tm), 0)
        cc = lax.broadcasted_iota(jnp.int32, (tm, tm), 1)
        tri = jnp.where((rr >= cc) & ((rr // q) == (cc // q)), 1.0, 0.0).astype(BF16)
        cs3 = _dot(tri, jnp.concatenate(_split3(dt * arow_ref[...]), axis=1))
        acs = cs3[:, 0:HP] + cs3[:, HP:2 * HP] + cs3[:, 2 * HP:3 * HP]
        acs_s[...] = acs
        st["dt"], st["acs"] = dt, acs

    def dt_factors():
        dt, acs = st["dt"], st["acs"]
        lasts = [acs[c * q + q - 1:c * q + q, :] for c in range(nch)]
        last_b = jnp.concatenate([jnp.broadcast_to(l, (q, HP)) for l in lasts], axis=0)
        eacs = jnp.exp2(acs)
        wend = dt * jnp.exp2(last_b - acs)
        dec = jnp.exp2(jnp.concatenate(lasts + [jnp.zeros((DEC_ROWS - nch, HP), F32)], axis=0))
        st["lhs"] = jnp.concatenate([eacs.astype(BF16), wend.astype(BF16), *_split3(dec)], axis=0)

    def pair_rows(name, dst):
        a = st[name]
        nxt = pltpu.roll(a, HP - 1, 1)
        z = jnp.concatenate(
            [p[c * q:(c + 1) * q] for c in range(nch) for p in (a, nxt)], axis=0)
        zt = _transpose_rows_padded(z)
        for c in range(nch):
            dst[c] = zt[:, c * w2:(c + 1) * w2]

    def proj(jb):
        cols = slice(jb * COL_BLK, (jb + 1) * COL_BLK)
        cbuf_s[CONV_PAD:CONV_PAD + tm, cols] = _dot(xn, wxbc_ref[:, cols])

    def conv(jb):
        cols = slice(jb * COL_BLK, (jb + 1) * COL_BLK)
        acc = convb_ref[:, cols] + (cbuf_s[CONV_PAD:CONV_PAD + tm, cols]
                                    * convw_ref[CONV_W - 1:CONV_W, cols])
        for j in range(CONV_W - 1):
            off = CONV_PAD - (CONV_W - 1) + j
            acc = acc + cbuf_s[off:off + tm, cols] * convw_ref[j:j + 1, cols]
        return _silu(acc)

    def x_block(jb):
        cols = slice(jb * COL_BLK, (jb + 1) * COL_BLK)
        xc = conv(jb)
        xe = jnp.where(even_head, xc, 0.0).astype(BF16)
        xo = jnp.where(even_head, 0.0, xc).astype(BF16)
        for c in range(nch):
            xbd_s[c, 0:q, cols] = xe[c * q:(c + 1) * q]
            xbd_s[c, q:w2, cols] = xo[c * q:(c + 1) * q]
        ex = _dot(st["lhs"], expand_ref[:, cols])
        eax_s[:, cols] = ex[0:tm].astype(BF16)
        xw_s[:, cols] = (xc * ex[tm:d0]).astype(BF16)
        decx_s[:, cols] = (ex[d0:d0 + DEC_ROWS] + ex[d0 + DEC_ROWS:d0 + 2 * DEC_ROWS]
                           + ex[d0 + 2 * DEC_ROWS:d0 + 3 * DEC_ROWS])

    def b_block(jb):
        btt = _transpose_rows_padded(conv(jb))
        r = slice((jb - n_xb) * COL_BLK, (jb - n_xb + 1) * COL_BLK)
        for c in range(nch):
            bt_s[c, r, :] = btt[:, c * q:(c + 1) * q].astype(BF16)

    def c_block(jb):
        r = slice((jb - n_xb - n_bb) * COL_BLK, (jb - n_xb - n_bb + 1) * COL_BLK)
        c_s[:, r] = conv(jb).astype(BF16)

    def q_block(jq):
        cols = slice(jq * COL_BLK, (jq + 1) * COL_BLK)
        qv = _rope(_dot(xn, wq_ref[:, cols]), cos_t, sa_t, sb_t) * (LOG2E * A_HEAD_DIM ** -0.5)
        per = COL_BLK // A_HEAD_DIM
        for hh in range(per):
            q_s[jq * per + hh] = qv[:, hh * A_HEAD_DIM:(hh + 1) * A_HEAD_DIM].astype(BF16)

    def kv_block():
        kv = _rope(_dot(xn, wk_ref[...]), cos_t, sa_t, sb_t)
        vv = _dot(xn, wv_ref[...])
        kwin_s[WINDOW:WINDOW + tm, :] = kv.astype(BF16)
        vwt_s[:, WINDOW:WINDOW + tm] = _transpose_rows_padded(vv).astype(BF16)
        ko_ref[0] = kv[tm - n_keep:, :]
        vo_ref[0] = vv[tm - n_keep:, :]

    P = functools.partial
    bc = list(range(n_xb, n_xb + 2 * n_bb))
    fin = [P(b_block, j) if j < n_xb + n_bb else P(c_block, j) for j in bc]
    light = [dt_cumsum, dt_factors, P(pair_rows, "acs", acst2_s), P(pair_rows, "dt", dtt2_s)]
    light += [P(q_block, jq) for jq in range(A_WIDTH // COL_BLK)] + [kv_block]
    order = [P(proj, bc[0])]
    blocks = bc + list(range(n_xb))
    fins = fin + [P(x_block, j) for j in range(n_xb)]
    for i in range(len(blocks)):
        if i + 1 < len(blocks):
            order.append(P(proj, blocks[i + 1]))
        order.append(fins[i])
        if light:
            order.append(light.pop(0))
    order += light
    for stage in order:
        stage()
    tail = cbuf_s[CONV_PAD + tm - (CONV_W - 1):CONV_PAD + tm, :]
    convo_ref[0] = tail
    cbuf_s[CONV_PAD - (CONV_W - 1):CONV_PAD, :] = tail

    first_half = lax.broadcasted_iota(jnp.int32, (q, w2), 1) < q
    row_g = lax.broadcasted_iota(jnp.int32, (q, PAIRS_PER_GROUP * w2), 0)
    pos_g = lax.broadcasted_iota(jnp.int32, (q, PAIRS_PER_GROUP * w2), 1) % q
    causal_g = row_g >= pos_g
    diag_g = row_g == pos_g
    if not has_state:
        krow = lax.broadcasted_iota(jnp.int32, (wlen, qa), 0)
        qcol = lax.broadcasted_iota(jnp.int32, (wlen, qa), 1)
        dchunk = krow // CHUNK - qcol // CHUNK
        band = (dchunk >= 0) & (dchunk <= WINDOW // CHUNK)
    sink_rows = [
        jnp.concatenate([jnp.full((1, qa), sink_ref[j * A_REP + r], F32) for r in range(A_REP)],
                        axis=1) for j in range(A_KV)]

    def ssd_group(c, g):
        rows = slice(c * q, (c + 1) * q)
        acs_c = acs_s[rows, :]
        bt_g = bt_s[c, g * D_STATE:(g + 1) * D_STATE, :]
        c_g = c_s[rows, g * D_STATE:(g + 1) * D_STATE]
        cb = _dot(c_g, bt_g)
        cb2 = jnp.concatenate([cb, cb], axis=1)
        ystate = _dot(c_g, ht_s[g].astype(BF16))
        gl = slice(g * GW, (g + 1) * GW)
        pairs = range(g * PAIRS_PER_GROUP, (g + 1) * PAIRS_PER_GROUP)
        col = jnp.concatenate(
            [jnp.where(first_half, acs_c[:, 2 * k:2 * k + 1], acs_c[:, 2 * k + 1:2 * k + 2])
             for k in pairs], axis=1)
        arow = jnp.concatenate([acst2_s[c, 2 * k:2 * k + 1, :] for k in pairs], axis=1)
        drow = jnp.concatenate([dtt2_s[c, 2 * k:2 * k + 1, :] for k in pairs], axis=1)
        dskw = jnp.concatenate([dsk_ref[k:k + 1, 0:w2] for k in pairs], axis=1)
        seg = jnp.where(causal_g, jnp.exp2(col - arow), 0.0)
        m2 = seg * (jnp.concatenate([cb2] * PAIRS_PER_GROUP, axis=1) * drow)
        m2 = (m2 + jnp.where(diag_g, dskw, 0.0)).astype(BF16)
        y2 = jnp.concatenate(
            [_dot(m2[:, kk * w2:(kk + 1) * w2], xbd_s[c, :, k * PAIR_W:(k + 1) * PAIR_W])
             for kk, k in enumerate(pairs)], axis=1)
        ym_s[rows, gl] = (y2 + ystate * eax_s[rows, gl].astype(F32)).astype(BF16)
        ht_s[g] = ht_s[g] * decx_s[c:c + 1, gl] + _dot(bt_g, xw_s[rows, gl])

    def attn_head(i, j):
        ra = i * qa
        rows = slice(ra, ra + qa)
        if not has_state:
            ok = band if ra >= WINDOW else band & ((krow >= WINDOW - ra) | (si > 0))
            bias = jnp.where(ok, 0.0, -jnp.inf)
            bias4 = jnp.concatenate([bias] * A_REP, axis=1)
        hs = slice(j * A_HEAD_DIM, (j + 1) * A_HEAD_DIM)
        qst = q_s[j * A_REP:(j + 1) * A_REP, rows, :].reshape(A_REP * qa, A_HEAD_DIM)
        st = _dot_nt(kwin_s[ra:ra + wlen, hs], qst)
        if not has_state:
            st = st + bias4
        m = jnp.maximum(jnp.max(st, axis=0, keepdims=True), sink_rows[j])
        e = jnp.exp2(st - m)
        den = jnp.sum(e, axis=0, keepdims=True) + jnp.exp2(sink_rows[j] - m)
        ot = _dot(vwt_s[hs, ra:ra + wlen], e.astype(BF16)) * (1.0 / den)
        for r2 in range(A_REP // 2):
            two = jnp.concatenate(
                [ot[:, (2 * r2) * qa:(2 * r2 + 1) * qa], ot[:, (2 * r2 + 1) * qa:(2 * r2 + 2) * qa]],
                axis=0)
            h0 = j * A_REP + 2 * r2
            at_s[rows, h0 * A_HEAD_DIM:(h0 + 2) * A_HEAD_DIM] = (
                _transpose_rows_padded(two).astype(BF16))

    def z_block(jb):
        cols = slice(jb * SIDE_BLK, (jb + 1) * SIDE_BLK)
        zs_s[:, cols] = _silu(_dot(xn_s[...], wz_ref[:, cols])).astype(BF16)

    def gate_block(w_ref, dst, jb):
        cols = slice(jb * SIDE_BLK, (jb + 1) * SIDE_BLK)
        dst[:, cols] = _sigmoid(_dot(xn_s[...], w_ref[:, cols])).astype(BF16)

    side_work = [functools.partial(z_block, jb) for jb in range(D_INNER // SIDE_BLK)]
    side_work += [functools.partial(gate_block, wgm_ref, gm_s, jb) for jb in range(D_MODEL // SIDE_BLK)]
    side_work += [functools.partial(gate_block, wga_ref, ga_s, jb) for jb in range(D_MODEL // SIDE_BLK)]

    def out_norm(i):
        rows = slice(i * qa, (i + 1) * qa)
        ym_s[rows, :] = _rms(ym_s[rows, :].astype(F32) * zs_s[rows, :].astype(F32),
                             mng_ref[...]).astype(BF16)

    def out_mix(i):
        rows = slice(i * qa, (i + 1) * qa)
        br_m = _dot(ym_s[rows, :], wbm_ref[:, 0:D_MODEL])
        br_a = _dot(at_s[rows, :], wba_ref[:, 0:D_MODEL])
        gm_s[rows, :] = (gm_s[rows, :].astype(F32) * br_m
                         + ga_s[rows, :].astype(F32) * br_a).astype(BF16)

    def out_proj(i):
        rows = slice(i * qa, (i + 1) * qa)
        y_ref[rows, :] = x_ref[rows, :] + _rms(_dot(gm_s[rows, :], wo_ref[:, 0:D_MODEL]),
                                               gpost_ref[...])

    scan = [functools.partial(ssd_group, c, g) for c in range(nch) for g in range(M_GROUPS)]
    heads = [functools.partial(attn_head, i, j) for i in range(tm // qa) for j in range(A_KV)]
    units = []
    for k, unit in enumerate(scan):
        units.append(unit)
        units += heads[min(len(heads), 2 * len(heads) * k // len(scan)):
                       min(len(heads), 2 * len(heads) * (k + 1) // len(scan))]
    done = 0
    for u, unit in enumerate(units):
        unit()
        upto = -(-len(side_work) * (u + 1) // len(units))
        for side in side_work[done:upto]:
            side()
        done = upto
    for i in range(tm // qa):
        out_norm(i)
    for i in range(tm // qa):
        out_mix(i)
    for i in range(tm // qa):
        out_proj(i)

    if not has_state:
        kwin_s[0:WINDOW, :] = kwin_s[tm:tm + WINDOW, :]
        vwt_s[:, 0:WINDOW] = vwt_s[:, tm:tm + WINDOW]

    @pl.when(si == ns - 1)
    def _():
        for h in range(M_HEADS):
            g, r = divmod(h, M_HPG)
            ssmo_ref[0, h] = _transpose_rows_padded(ht_s[g, :, r * M_HEAD_DIM:(r + 1) * M_HEAD_DIM])


def _mixer(x3d, tables, state, wts, *, tm, q, qa, n_keep):
    bsz, seq, _ = x3d.shape
    assert seq % tm == 0 and tm % qa == 0 and qa % q == 0 and q % BF16_ROWS == 0
    assert tm >= CONV_PAD >= CONV_W - 1
    ns = seq // tm
    nch = tm // q
    has_state = state is not None
    if has_state:
        assert ns == 1 and n_keep == tm and qa == tm
    else:
        assert tm >= WINDOW and n_keep == WINDOW and seq >= WINDOW and q == CHUNK and qa % CHUNK == 0

    in_specs = [pl.BlockSpec((None, tm, D_MODEL), lambda b, s: (b, s, 0))]
    in_specs += [pl.BlockSpec((tm, LANES), lambda b, s: (s, 0))] * 3
    args = [x3d, *tables]
    if has_state:
        conv0, ssm0, ck, cv = state
        assert ck.shape[1] == WINDOW
        in_specs += [
            pl.BlockSpec((1, CONV_W - 1, CONV_DIM), lambda b, s: (b, 0, 0)),
            pl.BlockSpec((1, M_HEADS, M_HEAD_DIM, D_STATE), lambda b, s: (b, 0, 0, 0)),
            pl.BlockSpec((1, WINDOW, KV_WIDTH), lambda b, s: (b, 0, 0)),
            pl.BlockSpec((1, WINDOW, KV_WIDTH), lambda b, s: (b, 0, 0)),
        ]
        args += [conv0, ssm0, ck, cv]
    for name, w in wts:
        if name == "sink":
            in_specs.append(pl.BlockSpec(memory_space=pltpu.SMEM))
        else:
            in_specs.append(_const_spec(w.shape))
        args.append(w)

    out_shape = [
        jax.ShapeDtypeStruct((bsz, seq, D_MODEL), F32),
        jax.ShapeDtypeStruct((bsz, CONV_W - 1, CONV_DIM), F32),
        jax.ShapeDtypeStruct((bsz, M_HEADS, M_HEAD_DIM, D_STATE), F32),
        jax.ShapeDtypeStruct((bsz, n_keep, KV_WIDTH), F32),
        jax.ShapeDtypeStruct((bsz, n_keep, KV_WIDTH), F32),
    ]
    out_specs = [
        pl.BlockSpec((None, tm, D_MODEL), lambda b, s: (b, s, 0)),
        pl.BlockSpec((1, CONV_W - 1, CONV_DIM), lambda b, s: (b, 0, 0)),
        pl.BlockSpec((1, M_HEADS, M_HEAD_DIM, D_STATE), lambda b, s: (b, 0, 0, 0)),
        pl.BlockSpec((1, n_keep, KV_WIDTH), lambda b, s: (b, 0, 0)),
        pl.BlockSpec((1, n_keep, KV_WIDTH), lambda b, s: (b, 0, 0)),
    ]
    w2 = 2 * q
    scratch = [
        pltpu.VMEM((CONV_PAD + tm, CONV_DIM), F32),
        pltpu.VMEM((nch, w2, D_INNER), BF16),
        pltpu.VMEM((tm, BC_W), BF16),
        pltpu.VMEM((nch, BC_W, q), BF16),
        pltpu.VMEM((tm, HP), F32),
        pltpu.VMEM((nch, HP, w2), F32),
        pltpu.VMEM((nch, HP, w2), F32),
        pltpu.VMEM((tm, D_INNER), BF16),
        pltpu.VMEM((tm, D_INNER), BF16),
        pltpu.VMEM((DEC_ROWS, D_INNER), F32),
        pltpu.VMEM((A_HEADS, tm, A_HEAD_DIM), BF16),
        pltpu.VMEM((WINDOW + tm, KV_WIDTH), BF16),
        pltpu.VMEM((KV_WIDTH, WINDOW + tm), BF16),
        pltpu.VMEM((M_GROUPS, D_STATE, GW), F32),
        pltpu.VMEM((tm, D_MODEL), BF16),
        pltpu.VMEM((tm, D_INNER), BF16),
        pltpu.VMEM((tm, A_WIDTH), BF16),
        pltpu.VMEM((tm, D_INNER), BF16),
        pltpu.VMEM((tm, D_MODEL), BF16),
        pltpu.VMEM((tm, D_MODEL), BF16),
    ]
    kern = functools.partial(_mixer_kernel, tm=tm, q=q, qa=qa, n_keep=n_keep, has_state=has_state)
    return pl.pallas_call(
        kern,
        grid=(bsz, ns),
        in_specs=in_specs,
        out_specs=out_specs,
        out_shape=out_shape,
        scratch_shapes=scratch,
        compiler_params=pltpu.CompilerParams(
            dimension_semantics=("arbitrary", "arbitrary"), vmem_limit_bytes=VMEM_LIMIT_BYTES),
        name="mixer_state" if has_state else "mixer_prompt",
    )(*args)


def _rope_tables(pos):
    half = ROT_DIM // 2
    inv_freq = ROPE_THETA ** (-jnp.arange(0, ROT_DIM, 2, dtype=F32) / ROT_DIM)
    ang = pos.astype(F32)[:, None] * inv_freq[None, :]
    cos, sin = jnp.cos(ang), jnp.sin(ang)
    n = pos.shape[0]
    ones = jnp.ones((n, A_HEAD_DIM - ROT_DIM), F32)
    zeros = jnp.zeros((n, A_HEAD_DIM - ROT_DIM), F32)
    zh = jnp.zeros((n, half), F32)
    cos_t = jnp.concatenate([cos, cos, ones], axis=1)
    sa_t = jnp.concatenate([-sin, zh, zeros], axis=1)
    sb_t = jnp.concatenate([zh, sin, zeros], axis=1)
    rep = LANES // A_HEAD_DIM
    return tuple(jnp.tile(t, (1, rep)) for t in (cos_t, sa_t, sb_t))


def _matmul_weight(w):
    w = w.astype(BF16)
    if (w.shape[1] // LANES) % 4 == 0:
        w = jnp.pad(w, ((0, 0), (0, LANES)))
    return w


def _pick_tile(n, pref):
    t = min(n, pref)
    while n % t:
        t //= 2
    return t


def kernel(x_prompt, x_sample, state_conv, state_ssm, cache_k, cache_v, ffn1_pre_g, ffn1_w_gu, ffn1_w_down, ffn1_post_g, mix_pre_g, w_in, conv_w, conv_b, dt_bias, a_log, d_skip, m_norm_g, attn_sink, w_br_m, w_br_a, w_o, mix_post_g, ffn2_pre_g, ffn2_w_gu, ffn2_w_down, ffn2_post_g):
    depth = w_in.shape[0]
    bp, lp, _ = x_prompt.shape
    bs, ls, _ = x_sample.shape
    pos_p = jnp.arange(lp, dtype=jnp.int32)
    pos_s = PAST_LEN + jnp.arange(ls, dtype=jnp.int32)
    tab_p = _rope_tables(pos_p)
    tab_s = _rope_tables(pos_s)
    expand = (jnp.arange(HP)[:, None] == (jnp.arange(D_INNER)[None, :] // M_HEAD_DIM)).astype(BF16)

    yp, ys = x_prompt, x_sample
    new_p, new_s = [], []
    for l in range(depth):
        def row(v):
            return v[l].reshape(1, -1).astype(F32)

        def ffn_w(w_gu, w_down):
            return (_matmul_weight(w_gu[l][:, :D_FF]), _matmul_weight(w_gu[l][:, D_FF:]),
                    _matmul_weight(w_down[l]))

        f1 = ffn_w(ffn1_w_gu, ffn1_w_down)
        f2 = ffn_w(ffn2_w_gu, ffn2_w_down)
        splits = np.cumsum(IN_SIZES)[:-1]
        wz, wxbc, wdt, wq, wk, wv, wgm, wga = jnp.split(w_in[l], splits, axis=1)
        pad_h = HP - M_HEADS
        wdt = jnp.pad(wdt, ((0, 0), (0, pad_h)))
        dtb = jnp.pad(dt_bias[l].astype(F32), (0, pad_h)).reshape(1, HP)
        arow = jnp.pad(-jnp.exp(a_log[l].astype(F32)) * LOG2E, (0, pad_h)).reshape(1, HP)
        def dsk_for(q):
            return jnp.repeat(d_skip[l].astype(F32).reshape(N_PAIRS, 2), q, axis=1)

        def wts_for(q):
            return [
                ("gpre", row(mix_pre_g)),
                ("wxbc", _matmul_weight(wxbc)), ("wdt", wdt.astype(BF16)),
                ("wq", _matmul_weight(wq)), ("wk", wk.astype(BF16)), ("wv", wv.astype(BF16)),
                ("convw", conv_w[l].astype(F32)), ("convb", row(conv_b)),
                ("dtb", dtb), ("arow", arow), ("dsk", dsk_for(q)), ("expand", _matmul_weight(expand)),
                ("sink", attn_sink[l].astype(F32) * LOG2E),
                ("wz", _matmul_weight(wz)), ("wgm", _matmul_weight(wgm)), ("wga", _matmul_weight(wga)),
                ("mng", row(m_norm_g)), ("wbm", _matmul_weight(w_br_m[l])),
                ("wba", _matmul_weight(w_br_a[l])), ("wo", _matmul_weight(w_o[l])),
                ("gpost", row(mix_post_g)),
            ]

        def layer(x3d, tables, state, tm_ffn, tm_mix, q, qa, n_keep):
            b, s, _ = x3d.shape
            x2d = x3d.reshape(b * s, D_MODEL)
            x2d = _ffn(x2d, row(ffn1_pre_g), *f1, row(ffn1_post_g), _pick_tile(b * s, tm_ffn))
            y3d, conv_o, ssm_o, k_o, v_o = _mixer(
                x2d.reshape(b, s, D_MODEL), tables, state, wts_for(q),
                tm=tm_mix, q=q, qa=qa, n_keep=n_keep)
            y2d = _ffn(y3d.reshape(b * s, D_MODEL), row(ffn2_pre_g), *f2, row(ffn2_post_g),
                       _pick_tile(b * s, tm_ffn))
            return (y2d.reshape(b, s, D_MODEL), conv_o, ssm_o,
                    k_o.reshape(b, n_keep, A_KV, A_HEAD_DIM), v_o.reshape(b, n_keep, A_KV, A_HEAD_DIM))

        rows_p = min(WINDOW, lp)
        yp, *st_p = layer(yp, tab_p, None, 1024, _pick_tile(lp, 256), CHUNK, 2 * CHUNK, rows_p)
        n_cache = cache_k.shape[2]
        def of_layer(a):
            return a.reshape(a.shape[1:]) if depth == 1 else a[l]

        state = (of_layer(state_conv).astype(F32), of_layer(state_ssm).astype(F32),
                 of_layer(cache_k).reshape(bs, n_cache, KV_WIDTH),
                 of_layer(cache_v).reshape(bs, n_cache, KV_WIDTH))
        q_s = min(ls, CHUNK)
        ys, *st_s = layer(ys, tab_s, state, 256, ls, q_s, ls, ls)
        new_p.append(st_p)
        new_s.append(st_s)

    def per_layer(t):
        return t[0][None] if depth == 1 else jnp.stack(t)

    conv_p, ssm_p, k_p, v_p = [per_layer(t) for t in zip(*new_p)]
    conv_s, ssm_s, k_s, v_s = [per_layer(t) for t in zip(*new_s)]
    return (yp, ys, conv_p, ssm_p, k_p, v_p, conv_s, ssm_s, k_s, v_s)
```

```python
import functools

import jax
import jax.numpy as jnp
import numpy as np
from jax import lax
from jax.experimental import pallas as pl
from jax.experimental.pallas import tpu as pltpu

D_MODEL = 1024
CHUNK = 64
EPS = 1e-6
D_INNER = 2 * D_MODEL
M_HEAD_DIM = 64
M_HEADS = D_INNER // M_HEAD_DIM
M_GROUPS = 4
M_HPG = M_HEADS // M_GROUPS
D_STATE = 128
CONV_W = 4
CONV_DIM = D_INNER + 2 * M_GROUPS * D_STATE
A_HEADS = 16
A_KV = 4
A_REP = A_HEADS // A_KV
A_HEAD_DIM = 64
A_WIDTH = A_HEADS * A_HEAD_DIM
KV_WIDTH = A_KV * A_HEAD_DIM
WINDOW = 128
ROT_DIM = A_HEAD_DIM // 4
ROPE_THETA = 500000.0
LOG2E = 1.4426950408889634
D_FF = 2816
PAST_LEN = 1024
IN_SIZES = (D_INNER, CONV_DIM, M_HEADS, A_WIDTH, KV_WIDTH, KV_WIDTH, D_MODEL, D_MODEL)

LANES = 128
SUBLANES = 8
BF16_ROWS = 16
VMEM_LIMIT_BYTES = 60 * 1024 * 1024

HP = LANES
GW = M_HPG * M_HEAD_DIM
BC_W = M_GROUPS * D_STATE
PAIR_W = 2 * M_HEAD_DIM
N_PAIRS = M_HEADS // 2
PAIRS_PER_GROUP = M_HPG // 2
CONV_PAD = SUBLANES
COL_BLK = 256
SIDE_BLK = 256
FFN_PART_ROWS = 256
DEC_ROWS = BF16_ROWS

BF16 = jnp.bfloat16
F32 = jnp.float32


def _dot(a, b):
    return jnp.dot(a, b, preferred_element_type=F32)


def _dot_nt(a, b):
    return lax.dot_general(a, b, (((1,), (1,)), ((), ())), preferred_element_type=F32)


def _rms(x, g):
    return x * lax.rsqrt(jnp.mean(x * x, axis=-1, keepdims=True) + EPS) * g


def _silu(x):
    h = 0.5 * x
    return h + h * jnp.tanh(h)


def _sigmoid(x):
    return 1.0 / (1.0 + jnp.exp(-x))


def _softplus(x):
    return jnp.maximum(x, 0.0) + jnp.log(1.0 + jnp.exp(-jnp.abs(x)))


def _split3(x):
    hi = x.astype(BF16)
    r = x - hi.astype(F32)
    mid = r.astype(BF16)
    lo = (r - mid.astype(F32)).astype(BF16)
    return hi, mid, lo


def _const_spec(shape):
    zeros = (0,) * len(shape)
    return pl.BlockSpec(shape, lambda *_: zeros, pipeline_mode=pl.Buffered(1))


def _ffn_kernel(x_ref, gpre_ref, wg_ref, wu_ref, wd_ref, gpost_ref, o_ref, *, parts):
    rows = x_ref.shape[0] // parts
    sl = [slice(p * rows, (p + 1) * rows) for p in range(parts)]
    xn = [_rms(x_ref[s, :], gpre_ref[...]).astype(BF16) for s in sl]
    act = []
    for p in range(parts):
        gate = _dot(xn[p], wg_ref[:, 0:D_FF])
        up = _dot(xn[p], wu_ref[:, 0:D_FF])
        act.append((_silu(gate) * up).astype(BF16))
    for p in range(parts):
        y = _dot(act[p], wd_ref[:, 0:D_MODEL])
        o_ref[sl[p], :] = x_ref[sl[p], :] + 0.5 * _rms(y, gpost_ref[...])


def _ffn(x2d, gpre, wg, wu, wd, gpost, tm):
    m = x2d.shape[0]
    assert m % tm == 0
    parts = max(1, tm // FFN_PART_ROWS)
    return pl.pallas_call(
        functools.partial(_ffn_kernel, parts=parts),
        grid=(m // tm,),
        in_specs=[
            pl.BlockSpec((tm, D_MODEL), lambda i: (i, 0)),
            _const_spec(gpre.shape), _const_spec(wg.shape), _const_spec(wu.shape),
            _const_spec(wd.shape), _const_spec(gpost.shape),
        ],
        out_specs=pl.BlockSpec((tm, D_MODEL), lambda i: (i, 0)),
        out_shape=jax.ShapeDtypeStruct((m, D_MODEL), F32),
        compiler_params=pltpu.CompilerParams(
            dimension_semantics=("arbitrary",), vmem_limit_bytes=VMEM_LIMIT_BYTES),
        name="ffn_half_step",
    )(x2d, gpre, wg, wu, wd, gpost)


def _rope(x, cos_t, sa_t, sb_t):
    w = x.shape[1]
    reps = w // LANES
    cos_f = jnp.tile(cos_t, (1, reps))
    sa_f = jnp.tile(sa_t, (1, reps))
    sb_f = jnp.tile(sb_t, (1, reps))
    half = ROT_DIM // 2
    return x * cos_f + pltpu.roll(x, w - half, 1) * sa_f + pltpu.roll(x, half, 1) * sb_f


def _transpose_rows_padded(x):
    r, c = x.shape
    rp = -(-r // LANES) * LANES
    cp = -(-c // LANES) * LANES
    if cp != c:
        x = jnp.concatenate([x, jnp.zeros((r, cp - c), x.dtype)], axis=1)
    if rp != r:
        x = jnp.concatenate([x, jnp.zeros((rp - r, cp), x.dtype)], axis=0)
    xt = x.T
    return xt if (rp == r and cp == c) else xt[:c, :r]


def _mixer_kernel(*refs, tm, q, qa, n_keep, has_state):
    it = iter(refs)
    x_ref = next(it)
    cos_ref, sa_ref, sb_ref = next(it), next(it), next(it)
    if has_state:
        conv0_ref, ssm0_ref, ck_ref, cv_ref = next(it), next(it), next(it), next(it)
    (gpre_ref, wxbc_ref, wdt_ref, wq_ref, wk_ref, wv_ref,
     convw_ref, convb_ref, dtb_ref, arow_ref, dsk_ref, expand_ref, sink_ref,
     wz_ref, wgm_ref, wga_ref, mng_ref, wbm_ref, wba_ref, wo_ref, gpost_ref) = [
         next(it) for _ in range(21)]
    y_ref, convo_ref, ssmo_ref, ko_ref, vo_ref = [next(it) for _ in range(5)]
    (cbuf_s, xbd_s, c_s, bt_s, acs_s, acst2_s, dtt2_s, eax_s, xw_s, decx_s,
     q_s, kwin_s, vwt_s, ht_s, xn_s, ym_s, at_s, zs_s, gm_s, ga_s) = [next(it) for _ in range(20)]

    si = pl.program_id(1)
    ns = pl.num_programs(1)
    nch = tm // q
    w2 = 2 * q
    wlen = WINDOW + qa
    assert nch <= DEC_ROWS

    @pl.when(si == 0)
    def _():
        if has_state:
            cbuf_s[CONV_PAD - (CONV_W - 1):CONV_PAD, :] = conv0_ref[0]
            for h in range(M_HEADS):
                g, r = divmod(h, M_HPG)
                ht_s[g, :, r * M_HEAD_DIM:(r + 1) * M_HEAD_DIM] = _transpose_rows_padded(ssm0_ref[0, h])
            kwin_s[0:WINDOW, :] = ck_ref[0].astype(BF16)
            vwt_s[:, 0:WINDOW] = _transpose_rows_padded(cv_ref[0]).astype(BF16)
        else:
            cbuf_s[0:CONV_PAD, :] = jnp.zeros((CONV_PAD, CONV_DIM), F32)
            ht_s[...] = jnp.zeros(ht_s.shape, F32)
            kwin_s[0:WINDOW, :] = jnp.zeros((WINDOW, KV_WIDTH), BF16)
            vwt_s[:, 0:WINDOW] = jnp.zeros((KV_WIDTH, WINDOW), BF16)

    x = x_ref[...]
    xn = _rms(x, gpre_ref[...]).astype(BF16)
    xn_s[...] = xn

    cos_t, sa_t, sb_t = cos_ref[...], sa_ref[...], sb_ref[...]
    lane_b = lax.broadcasted_iota(jnp.int32, (tm, COL_BLK), 1)
    even_head = (lane_b % PAIR_W) < M_HEAD_DIM
    d0 = 2 * tm
    n_xb = D_INNER // COL_BLK
    n_bb = BC_W // COL_BLK
    st = {}

    def dt_cumsum():
        dt = _softplus(_dot(xn, wdt_ref[...]) + dtb_ref[...])
        rr = lax.broadcasted_iota(jnp.int32, (tm, tm), 0)
        cc = lax.broadcasted_iota(jnp.int32, (tm, tm), 1)
        tri = jnp.where((rr >= cc) & ((rr // q) == (cc // q)), 1.0, 0.0).astype(BF16)
        cs3 = _dot(tri, jnp.concatenate(_split3(dt * arow_ref[...]), axis=1))
        acs = cs3[:, 0:HP] + cs3[:, HP:2 * HP] + cs3[:, 2 * HP:3 * HP]
        acs_s[...] = acs
        st["dt"], st["acs"] = dt, acs

    def dt_factors():
        dt, acs = st["dt"], st["acs"]
        lasts = [acs[c * q + q - 1:c * q + q, :] for c in range(nch)]
        last_b = jnp.concatenate([jnp.broadcast_to(l, (q, HP)) for l in lasts], axis=0)
        eacs = jnp.exp2(acs)
        wend = dt * jnp.exp2(last_b - acs)
        dec = jnp.exp2(jnp.concatenate(lasts + [jnp.zeros((DEC_ROWS - nch, HP), F32)], axis=0))
        st["lhs"] = jnp.concatenate([eacs.astype(BF16), wend.astype(BF16), *_split3(dec)], axis=0)

    def pair_rows(name, dst):
        a = st[name]
        nxt = pltpu.roll(a, HP - 1, 1)
        z = jnp.concatenate(
            [p[c * q:(c + 1) * q] for c in range(nch) for p in (a, nxt)], axis=0)
        zt = _transpose_rows_padded(z)
        for c in range(nch):
            dst[c] = zt[:, c * w2:(c + 1) * w2]

    def proj(jb):
        cols = slice(jb * COL_BLK, (jb + 1) * COL_BLK)
        cbuf_s[CONV_PAD:CONV_PAD + tm, cols] = _dot(xn, wxbc_ref[:, cols])

    def conv(jb):
        cols = slice(jb * COL_BLK, (jb + 1) * COL_BLK)
        acc = convb_ref[:, cols] + (cbuf_s[CONV_PAD:CONV_PAD + tm, cols]
                                    * convw_ref[CONV_W - 1:CONV_W, cols])
        for j in range(CONV_W - 1):
            off = CONV_PAD - (CONV_W - 1) + j
            acc = acc + cbuf_s[off:off + tm, cols] * convw_ref[j:j + 1, cols]
        return _silu(acc)

    def x_block(jb):
        cols = slice(jb * COL_BLK, (jb + 1) * COL_BLK)
        xc = conv(jb)
        xe = jnp.where(even_head, xc, 0.0).astype(BF16)
        xo = jnp.where(even_head, 0.0, xc).astype(BF16)
        for c in range(nch):
            xbd_s[c, 0:q, cols] = xe[c * q:(c + 1) * q]
            xbd_s[c, q:w2, cols] = xo[c * q:(c + 1) * q]
        ex = _dot(st["lhs"], expand_ref[:, cols])
        eax_s[:, cols] = ex[0:tm].astype(BF16)
        xw_s[:, cols] = (xc * ex[tm:d0]).astype(BF16)
        decx_s[:, cols] = (ex[d0:d0 + DEC_ROWS] + ex[d0 + DEC_ROWS:d0 + 2 * DEC_ROWS]
                           + ex[d0 + 2 * DEC_ROWS:d0 + 3 * DEC_ROWS])

    def b_block(jb):
        btt = _transpose_rows_padded(conv(jb))
        r = slice((jb - n_xb) * COL_BLK, (jb - n_xb + 1) * COL_BLK)
        for c in range(nch):
            bt_s[c, r, :] = btt[:, c * q:(c + 1) * q].astype(BF16)

    def c_block(jb):
        r = slice((jb - n_xb - n_bb) * COL_BLK, (jb - n_xb - n_bb + 1) * COL_BLK)
        c_s[:, r] = conv(jb).astype(BF16)

    def q_block(jq):
        cols = slice(jq * COL_BLK, (jq + 1) * COL_BLK)
        qv = _rope(_dot(xn, wq_ref[:, cols]), cos_t, sa_t, sb_t) * (LOG2E * A_HEAD_DIM ** -0.5)
        per = COL_BLK // A_HEAD_DIM
        for hh in range(per):
            q_s[jq * per + hh] = qv[:, hh * A_HEAD_DIM:(hh + 1) * A_HEAD_DIM].astype(BF16)

    def kv_block():
        kv = _rope(_dot(xn, wk_ref[...]), cos_t, sa_t, sb_t)
        vv = _dot(xn, wv_ref[...])
        kwin_s[WINDOW:WINDOW + tm, :] = kv.astype(BF16)
        vwt_s[:, WINDOW:WINDOW + tm] = _transpose_rows_padded(vv).astype(BF16)
        ko_ref[0] = kv[tm - n_keep:, :]
        vo_ref[0] = vv[tm - n_keep:, :]

    P = functools.partial
    bc = list(range(n_xb, n_xb + 2 * n_bb))
    fin = [P(b_block, j) if j < n_xb + n_bb else P(c_block, j) for j in bc]
    light = [dt_cumsum, dt_factors, P(pair_rows, "acs", acst2_s), P(pair_rows, "dt", dtt2_s)]
    light += [P(q_block, jq) for jq in range(A_WIDTH // COL_BLK)] + [kv_block]
    order = [P(proj, bc[0])]
    blocks = bc + list(range(n_xb))
    fins = fin + [P(x_block, j) for j in range(n_xb)]
    for i in range(len(blocks)):
        if i + 1 < len(blocks):
            order.append(P(proj, blocks[i + 1]))
        order.append(fins[i])
        if light:
            order.append(light.pop(0))
    order += light
    for stage in order:
        stage()
    tail = cbuf_s[CONV_PAD + tm - (CONV_W - 1):CONV_PAD + tm, :]
    convo_ref[0] = tail
    cbuf_s[CONV_PAD - (CONV_W - 1):CONV_PAD, :] = tail

    first_half = lax.broadcasted_iota(jnp.int32, (q, w2), 1) < q
    row_g = lax.broadcasted_iota(jnp.int32, (q, PAIRS_PER_GROUP * w2), 0)
    pos_g = lax.broadcasted_iota(jnp.int32, (q, PAIRS_PER_GROUP * w2), 1) % q
    causal_g = row_g >= pos_g
    diag_g = row_g == pos_g
    if not has_state:
        krow = lax.broadcasted_iota(jnp.int32, (wlen, qa), 0)
        qcol = lax.broadcasted_iota(jnp.int32, (wlen, qa), 1)
        dchunk = krow // CHUNK - qcol // CHUNK
        band = (dchunk >= 0) & (dchunk <= WINDOW // CHUNK)
    sink_rows = [
        jnp.concatenate([jnp.full((1, qa), sink_ref[j * A_REP + r], F32) for r in range(A_REP)],
                        axis=1) for j in range(A_KV)]

    def ssd_group(c, g):
        rows = slice(c * q, (c + 1) * q)
        acs_c = acs_s[rows, :]
        bt_g = bt_s[c, g * D_STATE:(g + 1) * D_STATE, :]
        c_g = c_s[rows, g * D_STATE:(g + 1) * D_STATE]
        cb = _dot(c_g, bt_g)
        cb2 = jnp.concatenate([cb, cb], axis=1)
        ystate = _dot(c_g, ht_s[g].astype(BF16))
        gl = slice(g * GW, (g + 1) * GW)
        pairs = range(g * PAIRS_PER_GROUP, (g + 1) * PAIRS_PER_GROUP)
        col = jnp.concatenate(
            [jnp.where(first_half, acs_c[:, 2 * k:2 * k + 1], acs_c[:, 2 * k + 1:2 * k + 2])
             for k in pairs], axis=1)
        arow = jnp.concatenate([acst2_s[c, 2 * k:2 * k + 1, :] for k in pairs], axis=1)
        drow = jnp.concatenate([dtt2_s[c, 2 * k:2 * k + 1, :] for k in pairs], axis=1)
        dskw = jnp.concatenate([dsk_ref[k:k + 1, 0:w2] for k in pairs], axis=1)
        seg = jnp.where(causal_g, jnp.exp2(col - arow), 0.0)
        m2 = seg * (jnp.concatenate([cb2] * PAIRS_PER_GROUP, axis=1) * drow)
        m2 = (m2 + jnp.where(diag_g, dskw, 0.0)).astype(BF16)
        y2 = jnp.concatenate(
            [_dot(m2[:, kk * w2:(kk + 1) * w2], xbd_s[c, :, k * PAIR_W:(k + 1) * PAIR_W])
             for kk, k in enumerate(pairs)], axis=1)
        ym_s[rows, gl] = (y2 + ystate * eax_s[rows, gl].astype(F32)).astype(BF16)
        ht_s[g] = ht_s[g] * decx_s[c:c + 1, gl] + _dot(bt_g, xw_s[rows, gl])

    def attn_head(i, j):
        ra = i * qa
        rows = slice(ra, ra + qa)
        if not has_state:
            ok = band if ra >= WINDOW else band & ((krow >= WINDOW - ra) | (si > 0))
            bias = jnp.where(ok, 0.0, -jnp.inf)
            bias4 = jnp.concatenate([bias] * A_REP, axis=1)
        hs = slice(j * A_HEAD_DIM, (j + 1) * A_HEAD_DIM)
        qst = q_s[j * A_REP:(j + 1) * A_REP, rows, :].reshape(A_REP * qa, A_HEAD_DIM)
        st = _dot_nt(kwin_s[ra:ra + wlen, hs], qst)
        if not has_state:
            st = st + bias4
        m = jnp.maximum(jnp.max(st, axis=0, keepdims=True), sink_rows[j])
        e = jnp.exp2(st - m)
        den = jnp.sum(e, axis=0, keepdims=True) + jnp.exp2(sink_rows[j] - m)
        ot = _dot(vwt_s[hs, ra:ra + wlen], e.astype(BF16)) * (1.0 / den)
        for r2 in range(A_REP // 2):
            two = jnp.concatenate(
                [ot[:, (2 * r2) * qa:(2 * r2 + 1) * qa], ot[:, (2 * r2 + 1) * qa:(2 * r2 + 2) * qa]],
                axis=0)
            h0 = j * A_REP + 2 * r2
            at_s[rows, h0 * A_HEAD_DIM:(h0 + 2) * A_HEAD_DIM] = (
                _transpose_rows_padded(two).astype(BF16))

    def z_block(jb):
        cols = slice(jb * SIDE_BLK, (jb + 1) * SIDE_BLK)
        zs_s[:, cols] = _silu(_dot(xn_s[...], wz_ref[:, cols])).astype(BF16)

    def gate_block(w_ref, dst, jb):
        cols = slice(jb * SIDE_BLK, (jb + 1) * SIDE_BLK)
        dst[:, cols] = _sigmoid(_dot(xn_s[...], w_ref[:, cols])).astype(BF16)

    side_work = [functools.partial(z_block, jb) for jb in range(D_INNER // SIDE_BLK)]
    side_work += [functools.partial(gate_block, wgm_ref, gm_s, jb) for jb in range(D_MODEL // SIDE_BLK)]
    side_work += [functools.partial(gate_block, wga_ref, ga_s, jb) for jb in range(D_MODEL // SIDE_BLK)]

    def out_norm(i):
        rows = slice(i * qa, (i + 1) * qa)
        ym_s[rows, :] = _rms(ym_s[rows, :].astype(F32) * zs_s[rows, :].astype(F32),
                             mng_ref[...]).astype(BF16)

    def out_mix(i):
        rows = slice(i * qa, (i + 1) * qa)
        br_m = _dot(ym_s[rows, :], wbm_ref[:, 0:D_MODEL])
        br_a = _dot(at_s[rows, :], wba_ref[:, 0:D_MODEL])
        gm_s[rows, :] = (gm_s[rows, :].astype(F32) * br_m
                         + ga_s[rows, :].astype(F32) * br_a).astype(BF16)

    def out_proj(i):
        rows = slice(i * qa, (i + 1) * qa)
        y_ref[rows, :] = x_ref[rows, :] + _rms(_dot(gm_s[rows, :], wo_ref[:, 0:D_MODEL]),
                                               gpost_ref[...])

    scan = [functools.partial(ssd_group, c, g) for c in range(nch) for g in range(M_GROUPS)]
    heads = [functools.partial(attn_head, i, j) for i in range(tm // qa) for j in range(A_KV)]
    units = []
    for k, unit in enumerate(scan):
        units.append(unit)
        units += heads[min(len(heads), 2 * len(heads) * k // len(scan)):
                       min(len(heads), 2 * len(heads) * (k + 1) // len(scan))]
    done = 0
    for u, unit in enumerate(units):
        upto = -(-len(side_work) * (u + 1) // len(units))
        for side in side_work[done:upto]:
            side()
        done = upto
        unit()
    for i in range(tm // qa):
        out_norm(i)
    for i in range(tm // qa):
        out_mix(i)
    for i in range(tm // qa):
        out_proj(i)

    if not has_state:
        kwin_s[0:WINDOW, :] = kwin_s[tm:tm + WINDOW, :]
        vwt_s[:, 0:WINDOW] = vwt_s[:, tm:tm + WINDOW]

    @pl.when(si == ns - 1)
    def _():
        for h in range(M_HEADS):
            g, r = divmod(h, M_HPG)
            ssmo_ref[0, h] = _transpose_rows_padded(ht_s[g, :, r * M_HEAD_DIM:(r + 1) * M_HEAD_DIM])


def _mixer(x3d, tables, state, wts, *, tm, q, qa, n_keep):
    bsz, seq, _ = x3d.shape
    assert seq % tm == 0 and tm % qa == 0 and qa % q == 0 and q % BF16_ROWS == 0
    assert tm >= CONV_PAD >= CONV_W - 1
    ns = seq // tm
    nch = tm // q
    has_state = state is not None
    if has_state:
        assert ns == 1 and n_keep == tm and qa == tm
    else:
        assert tm >= WINDOW and n_keep == WINDOW and seq >= WINDOW and q == CHUNK and qa % CHUNK == 0

    in_specs = [pl.BlockSpec((None, tm, D_MODEL), lambda b, s: (b, s, 0))]
    in_specs += [pl.BlockSpec((tm, LANES), lambda b, s: (s, 0))] * 3
    args = [x3d, *tables]
    if has_state:
        conv0, ssm0, ck, cv = state
        assert ck.shape[1] == WINDOW
        in_specs += [
            pl.BlockSpec((1, CONV_W - 1, CONV_DIM), lambda b, s: (b, 0, 0)),
            pl.BlockSpec((1, M_HEADS, M_HEAD_DIM, D_STATE), lambda b, s: (b, 0, 0, 0)),
            pl.BlockSpec((1, WINDOW, KV_WIDTH), lambda b, s: (b, 0, 0)),
            pl.BlockSpec((1, WINDOW, KV_WIDTH), lambda b, s: (b, 0, 0)),
        ]
        args += [conv0, ssm0, ck, cv]
    for name, w in wts:
        if name == "sink":
            in_specs.append(pl.BlockSpec(memory_space=pltpu.SMEM))
        else:
            in_specs.append(_const_spec(w.shape))
        args.append(w)

    out_shape = [
        jax.ShapeDtypeStruct((bsz, seq, D_MODEL), F32),
        jax.ShapeDtypeStruct((bsz, CONV_W - 1, CONV_DIM), F32),
        jax.ShapeDtypeStruct((bsz, M_HEADS, M_HEAD_DIM, D_STATE), F32),
        jax.ShapeDtypeStruct((bsz, n_keep, KV_WIDTH), F32),
        jax.ShapeDtypeStruct((bsz, n_keep, KV_WIDTH), F32),
    ]
    out_specs = [
        pl.BlockSpec((None, tm, D_MODEL), lambda b, s: (b, s, 0)),
        pl.BlockSpec((1, CONV_W - 1, CONV_DIM), lambda b, s: (b, 0, 0)),
        pl.BlockSpec((1, M_HEADS, M_HEAD_DIM, D_STATE), lambda b, s: (b, 0, 0, 0)),
        pl.BlockSpec((1, n_keep, KV_WIDTH), lambda b, s: (b, 0, 0)),
        pl.BlockSpec((1, n_keep, KV_WIDTH), lambda b, s: (b, 0, 0)),
    ]
    w2 = 2 * q
    scratch = [
        pltpu.VMEM((CONV_PAD + tm, CONV_DIM), F32),
        pltpu.VMEM((nch, w2, D_INNER), BF16),
        pltpu.VMEM((tm, BC_W), BF16),
        pltpu.VMEM((nch, BC_W, q), BF16),
        pltpu.VMEM((tm, HP), F32),
        pltpu.VMEM((nch, HP, w2), F32),
        pltpu.VMEM((nch, HP, w2), F32),
        pltpu.VMEM((tm, D_INNER), BF16),
        pltpu.VMEM((tm, D_INNER), BF16),
        pltpu.VMEM((DEC_ROWS, D_INNER), F32),
        pltpu.VMEM((A_HEADS, tm, A_HEAD_DIM), BF16),
        pltpu.VMEM((WINDOW + tm, KV_WIDTH), BF16),
        pltpu.VMEM((KV_WIDTH, WINDOW + tm), BF16),
        pltpu.VMEM((M_GROUPS, D_STATE, GW), F32),
        pltpu.VMEM((tm, D_MODEL), BF16),
        pltpu.VMEM((tm, D_INNER), BF16),
        pltpu.VMEM((tm, A_WIDTH), BF16),
        pltpu.VMEM((tm, D_INNER), BF16),
        pltpu.VMEM((tm, D_MODEL), BF16),
        pltpu.VMEM((tm, D_MODEL), BF16),
    ]
    kern = functools.partial(_mixer_kernel, tm=tm, q=q, qa=qa, n_keep=n_keep, has_state=has_state)
    return pl.pallas_call(
        kern,
        grid=(bsz, ns),
        in_specs=in_specs,
        out_specs=out_specs,
        out_shape=out_shape,
        scratch_shapes=scratch,
        compiler_params=pltpu.CompilerParams(
            dimension_semantics=("arbitrary", "arbitrary"), vmem_limit_bytes=VMEM_LIMIT_BYTES),
        name="mixer_state" if has_state else "mixer_prompt",
    )(*args)


def _rope_tables(pos):
    half = ROT_DIM // 2
    inv_freq = ROPE_THETA ** (-jnp.arange(0, ROT_DIM, 2, dtype=F32) / ROT_DIM)
    ang = pos.astype(F32)[:, None] * inv_freq[None, :]
    cos, sin = jnp.cos(ang), jnp.sin(ang)
    n = pos.shape[0]
    ones = jnp.ones((n, A_HEAD_DIM - ROT_DIM), F32)
    zeros = jnp.zeros((n, A_HEAD_DIM - ROT_DIM), F32)
    zh = jnp.zeros((n, half), F32)
    cos_t = jnp.concatenate([cos, cos, ones], axis=1)
    sa_t = jnp.concatenate([-sin, zh, zeros], axis=1)
    sb_t = jnp.concatenate([zh, sin, zeros], axis=1)
    rep = LANES // A_HEAD_DIM
    return tuple(jnp.tile(t, (1, rep)) for t in (cos_t, sa_t, sb_t))


def _matmul_weight(w):
    w = w.astype(BF16)
    if (w.shape[1] // LANES) % 4 == 0:
        w = jnp.pad(w, ((0, 0), (0, LANES)))
    return w


def _pick_tile(n, pref):
    t = min(n, pref)
    while n % t:
        t //= 2
    return t


def kernel(x_prompt, x_sample, state_conv, state_ssm, cache_k, cache_v, ffn1_pre_g, ffn1_w_gu, ffn1_w_down, ffn1_post_g, mix_pre_g, w_in, conv_w, conv_b, dt_bias, a_log, d_skip, m_norm_g, attn_sink, w_br_m, w_br_a, w_o, mix_post_g, ffn2_pre_g, ffn2_w_gu, ffn2_w_down, ffn2_post_g):
    depth = w_in.shape[0]
    bp, lp, _ = x_prompt.shape
    bs, ls, _ = x_sample.shape
    pos_p = jnp.arange(lp, dtype=jnp.int32)
    pos_s = PAST_LEN + jnp.arange(ls, dtype=jnp.int32)
    tab_p = _rope_tables(pos_p)
    tab_s = _rope_tables(pos_s)
    expand = (jnp.arange(HP)[:, None] == (jnp.arange(D_INNER)[None, :] // M_HEAD_DIM)).astype(BF16)

    yp, ys = x_prompt, x_sample
    new_p, new_s = [], []
    for l in range(depth):
        def row(v):
            return v[l].reshape(1, -1).astype(F32)

        def ffn_w(w_gu, w_down):
            return (_matmul_weight(w_gu[l][:, :D_FF]), _matmul_weight(w_gu[l][:, D_FF:]),
                    _matmul_weight(w_down[l]))

        f1 = ffn_w(ffn1_w_gu, ffn1_w_down)
        f2 = ffn_w(ffn2_w_gu, ffn2_w_down)
        splits = np.cumsum(IN_SIZES)[:-1]
        wz, wxbc, wdt, wq, wk, wv, wgm, wga = jnp.split(w_in[l], splits, axis=1)
        pad_h = HP - M_HEADS
        wdt = jnp.pad(wdt, ((0, 0), (0, pad_h)))
        dtb = jnp.pad(dt_bias[l].astype(F32), (0, pad_h)).reshape(1, HP)
        arow = jnp.pad(-jnp.exp(a_log[l].astype(F32)) * LOG2E, (0, pad_h)).reshape(1, HP)
        def dsk_for(q):
            return jnp.repeat(d_skip[l].astype(F32).reshape(N_PAIRS, 2), q, axis=1)

        def wts_for(q):
            return [
                ("gpre", row(mix_pre_g)),
                ("wxbc", _matmul_weight(wxbc)), ("wdt", wdt.astype(BF16)),
                ("wq", _matmul_weight(wq)), ("wk", wk.astype(BF16)), ("wv", wv.astype(BF16)),
                ("convw", conv_w[l].astype(F32)), ("convb", row(conv_b)),
                ("dtb", dtb), ("arow", arow), ("dsk", dsk_for(q)), ("expand", _matmul_weight(expand)),
                ("sink", attn_sink[l].astype(F32) * LOG2E),
                ("wz", _matmul_weight(wz)), ("wgm", _matmul_weight(wgm)), ("wga", _matmul_weight(wga)),
                ("mng", row(m_norm_g)), ("wbm", _matmul_weight(w_br_m[l])),
                ("wba", _matmul_weight(w_br_a[l])), ("wo", _matmul_weight(w_o[l])),
                ("gpost", row(mix_post_g)),
            ]

        def layer(x3d, tables, state, tm_ffn, tm_mix, q, qa, n_keep):
            b, s, _ = x3d.shape
            x2d = x3d.reshape(b * s, D_MODEL)
            x2d = _ffn(x2d, row(ffn1_pre_g), *f1, row(ffn1_post_g), _pick_tile(b * s, tm_ffn))
            y3d, conv_o, ssm_o, k_o, v_o = _mixer(
                x2d.reshape(b, s, D_MODEL), tables, state, wts_for(q),
                tm=tm_mix, q=q, qa=qa, n_keep=n_keep)
            y2d = _ffn(y3d.reshape(b * s, D_MODEL), row(ffn2_pre_g), *f2, row(ffn2_post_g),
                       _pick_tile(b * s, tm_ffn))
            return (y2d.reshape(b, s, D_MODEL), conv_o, ssm_o,
                    k_o.reshape(b, n_keep, A_KV, A_HEAD_DIM), v_o.reshape(b, n_keep, A_KV, A_HEAD_DIM))

        rows_p = min(WINDOW, lp)
        yp, *st_p = layer(yp, tab_p, None, 1024, _pick_tile(lp, 256), CHUNK, 2 * CHUNK, rows_p)
        n_cache = cache_k.shape[2]
        def of_layer(a):
            return a.reshape(a.shape[1:]) if depth == 1 else a[l]

        state = (of_layer(state_conv).astype(F32), of_layer(state_ssm).astype(F32),
                 of_layer(cache_k).reshape(bs, n_cache, KV_WIDTH),
                 of_layer(cache_v).reshape(bs, n_cache, KV_WIDTH))
        q_s = min(ls, CHUNK)
        ys, *st_s = layer(ys, tab_s, state, 256, ls, q_s, ls, ls)
        new_p.append(st_p)
        new_s.append(st_s)

    def per_layer(t):
        return t[0][None] if depth == 1 else jnp.stack(t)

    conv_p, ssm_p, k_p, v_p = [per_layer(t) for t in zip(*new_p)]
    conv_s, ssm_s, k_s, v_s = [per_layer(t) for t in zip(*new_s)]
    return (yp, ys, conv_p, ssm_p, k_p, v_p, conv_s, ssm_s, k_s, v_s)
```
